```python
import math
import jax, jax.numpy as jnp
from jax import lax
import numpy as np

D_MODEL = 1024
BATCH = 2
SEQ = 8192
DEPTH = 2
DEC_BATCH = 128
DEC_SEQ = 4
PAST_LEN = 8192
PAGE_SIZE = 128

HEAD_DIM = 64
Q_BLOCK = 128
SB_HEADS = 4
SB_KV_HEADS = 2
NSA_HEADS = 4
CMP_STRIDE = 16
CMP_LEN = 2 * CMP_STRIDE
CMP_HIDDEN = 2 * HEAD_DIM
SEL_BLOCK = 64
SEL_TOP = 16
WINDOW = 512
MLA_HEADS = 4
Q_LORA = 256
KV_LORA = 128
NOPE_DIM = 64
ROPE_DIM = 32
V_DIM = 64
ROPE_THETA = 10000.0
MOBA_HEADS = 4
MOBA_KV_HEADS = 2
MOBA_BLOCK = 256
MOBA_TOP = 3
NUM_BUCKETS = 32
MAX_DISTANCE = 128
N_BIAS_HEADS = NSA_HEADS + MOBA_HEADS
N_BRANCH = 4
BR_WIDTH = SB_HEADS * HEAD_DIM
D_FF = -(-8 * D_MODEL // (3 * 256)) * 256
ALPHA = (2 * DEPTH) ** 0.25
BETA_INIT = (8 * DEPTH) ** -0.25
IN_SIZES = (SB_HEADS * HEAD_DIM, SB_KV_HEADS * 2 * HEAD_DIM,
            NSA_HEADS * HEAD_DIM, 3 * 2 * HEAD_DIM, 3 * NSA_HEADS,
            Q_LORA, KV_LORA, ROPE_DIM,
            MOBA_HEADS * HEAD_DIM, MOBA_KV_HEADS * 2 * HEAD_DIM,
            N_BRANCH * D_MODEL)
N_IN = sum(IN_SIZES)

kernel_name = 'gated_parallel_sb_nsa_mla_moba_decoder_step'


def ein(spec, a, b):
    return jnp.einsum(spec, a, b, preferred_element_type=jnp.float32)


def layer_norm(x, g, b, eps=1e-5):
    xf = x.astype(jnp.float32)
    mu = xf.mean(-1, keepdims=True)
    var = jnp.square(xf - mu).mean(-1, keepdims=True)
    return ((xf - mu) * lax.rsqrt(var + eps) * g + b).astype(x.dtype)


def rms_norm(x, g, eps=1e-6):
    xf = x.astype(jnp.float32)
    return (xf * lax.rsqrt(jnp.mean(xf * xf, -1, keepdims=True) + eps) * g).astype(x.dtype)


def rope(x, pos):
    half = ROPE_DIM // 2
    inv = ROPE_THETA ** (-jnp.arange(half, dtype=jnp.float32) / half)
    ang = pos.astype(jnp.float32)[:, None] * inv
    ang = ang.reshape((1, pos.shape[0]) + (1,) * (x.ndim - 3) + (half,))
    cos, sin = jnp.cos(ang), jnp.sin(ang)
    xf = x.astype(jnp.float32)
    x1, x2 = xf[..., :half], xf[..., half:]
    return jnp.concatenate([x1 * cos - x2 * sin, x2 * cos + x1 * sin], -1).astype(x.dtype)


def t5_bucket(rel):
    n = jnp.maximum(rel, 0)
    exact = NUM_BUCKETS // 2
    far = exact + (jnp.log(jnp.maximum(n, 1).astype(jnp.float32) / exact)
                   / math.log(MAX_DISTANCE / exact) * (NUM_BUCKETS - exact)).astype(jnp.int32)
    return jnp.where(n < exact, n, jnp.minimum(far, NUM_BUCKETS - 1))


def t5_bias(table, rel, heads):
    return table[t5_bucket(rel), heads].astype(jnp.float32)


def masked_softmax(logits, mask):
    l = jnp.where(mask, logits, -jnp.inf)
    m = jnp.max(l, axis=-1, keepdims=True)
    m = jnp.where(jnp.isfinite(m), m, 0.0)
    e = jnp.where(mask, jnp.exp(l - m), 0.0)
    return e / jnp.maximum(e.sum(-1, keepdims=True), 1e-30)


def sweep(block_fn, n_q):
    qb = min(Q_BLOCK, n_q)
    out = lax.map(lambda i: block_fn(i * qb, qb), jnp.arange(n_q // qb))
    return jnp.moveaxis(out, 0, 1).reshape((out.shape[1], n_q) + out.shape[3:])


def flat_pages(r):
    return r.reshape((r.shape[0], r.shape[1] * r.shape[2]) + r.shape[3:])


def dense_fetch(rows, pos, grp):
    n = jnp.arange(rows.shape[0]).reshape((rows.shape[0],) + (1,) * (pos.ndim - 1))
    return rows[n, jnp.clip(pos, 0, rows.shape[1] - 1), grp]


def paged_fetch(pool, l, page_table, new_rows, pos, grp):
    n_seq = page_table.shape[0]
    past_len = page_table.shape[1] * PAGE_SIZE
    n = jnp.arange(n_seq).reshape((n_seq,) + (1,) * (pos.ndim - 1))
    pc = jnp.clip(pos, 0, past_len - 1)
    phys = page_table[n, pc // PAGE_SIZE]
    past = pool[l, phys, pc % PAGE_SIZE, grp]
    new = new_rows[n, jnp.clip(pos - past_len, 0, new_rows.shape[1] - 1), grp]
    return jnp.where((pos < past_len)[..., None, None], past, new)


def stick_breaking(q, kv, q_start):
    N, T = q.shape[:2]
    G = SB_HEADS // SB_KV_HEADS
    k, v = kv[..., 0, :], kv[..., 1, :]
    kpos = jnp.arange(kv.shape[1])
    scale = HEAD_DIM ** -0.5

    def block(start, qb):
        qblk = lax.dynamic_slice_in_dim(q, start, qb, axis=1).reshape(N, qb, SB_KV_HEADS, G, HEAD_DIM)
        qpos = q_start + start + jnp.arange(qb)
        z = ein('nqkgd,nskd->nkgqs', qblk, k) * scale
        mask = kpos[None, :] < qpos[:, None]
        log_rest = jnp.where(mask, jax.nn.log_sigmoid(-z), 0.0)
        tail = lax.cumsum(log_rest, axis=4, reverse=True) - log_rest
        wts = jnp.where(mask, jnp.exp(jax.nn.log_sigmoid(z) + tail), 0.0)
        o = ein('nkgqs,nskd->nqkgd', wts.astype(v.dtype), v)
        return o.reshape(N, qb, SB_HEADS, HEAD_DIM).astype(q.dtype)

    return sweep(block, T)


def nsa_compress(rows, pe, w1, w2):
    N, Tk = rows.shape[:2]
    nch = Tk // CMP_STRIDE
    ch = rows[:, :nch * CMP_STRIDE].reshape(N, nch, CMP_STRIDE, 2, HEAD_DIM)
    w1h = w1.reshape(2, 2, CMP_STRIDE, HEAD_DIM, CMP_HIDDEN)
    first = ein('njsck,cskh->njch', ch, w1h[:, 0])
    second = ein('njsck,cskh->njch', ch, w1h[:, 1])
    pe_term = ein('cf,cfh->ch', pe.reshape(2, CMP_LEN * HEAD_DIM), w1)
    h = jax.nn.gelu(first[:, :-1] + second[:, 1:] + pe_term)
    return ein('njch,chd->njcd', h.astype(rows.dtype), w2).astype(rows.dtype)


def nsa(q, gates, cmp_kv, sel_fetch, window_rows, q_start, table):
    N, T = q.shape[:2]
    Tk = q_start + T
    J = cmp_kv.shape[1]
    cstart = jnp.arange(J) * CMP_STRIDE
    cend = cstart + CMP_LEN - 1
    NB = -(-Tk // SEL_BLOCK)
    bidx = jnp.arange(NB)
    bstart = bidx * SEL_BLOCK
    cover = ((cstart[:, None] < bstart[None, :] + SEL_BLOCK) & (cend[:, None] >= bstart[None, :])).astype(jnp.float32)
    n_top = min(SEL_TOP, NB)
    hidx = jnp.arange(NSA_HEADS)
    off = jnp.arange(SEL_BLOCK)
    scale = HEAD_DIM ** -0.5

    def block(start, qb):
        qblk = lax.dynamic_slice_in_dim(q, start, qb, axis=1)
        g = lax.dynamic_slice_in_dim(gates, start, qb, axis=1)
        qpos = q_start + start + jnp.arange(qb)
        rel_c = qpos[:, None] - cend[None, :]
        lc = ein('nqhd,njd->nqhj', qblk, cmp_kv[:, :, 0]) * scale + t5_bias(table, rel_c[:, None, :], hidx[:, None])
        pc = masked_softmax(lc, (rel_c >= 0)[:, None, :])
        o_c = ein('nqhj,njd->nqhd', pc.astype(cmp_kv.dtype), cmp_kv[:, :, 1])
        imp = jnp.einsum('nqj,jb->nqb', pc.sum(axis=2), cover)
        cur = (qpos // SEL_BLOCK)[:, None]
        forced = (bidx[None, :] == 0) | (bidx[None, :] == cur) | (bidx[None, :] == cur - 1)
        score = jnp.where(forced, jnp.inf, jnp.where(bstart[None, :] <= qpos[:, None], imp, -jnp.inf))
        _, sel = lax.top_k(score, n_top)
        pos = (sel[..., None] * SEL_BLOCK + off).reshape(N, qb, n_top * SEL_BLOCK)
        kv_s = sel_fetch(pos)
        rel_s = qpos[None, :, None] - pos
        ls = ein('nqhd,nqmd->nqhm', qblk, kv_s[..., 0, :]) * scale + t5_bias(table, rel_s[:, :, None, :], hidx[:, None])
        ps = masked_softmax(ls, (rel_s >= 0)[:, :, None, :])
        o_s = ein('nqhm,nqmd->nqhd', ps.astype(kv_s.dtype), kv_s[..., 1, :])
        kw, kwpos = window_rows(start, qb)
        rel_w = qpos[:, None] - kwpos[None, :]
        lw = ein('nqhd,nwd->nqhw', qblk, kw[..., 0, :]) * scale + t5_bias(table, rel_w[:, None, :], hidx[:, None])
        mw = (rel_w >= 0) & (rel_w < WINDOW) & (kwpos[None, :] >= 0)
        pw = masked_softmax(lw, mw[:, None, :])
        o_w = ein('nqhw,nwd->nqhd', pw.astype(kw.dtype), kw[..., 1, :])
        o = g[:, :, 0, :, None] * o_c + g[:, :, 1, :, None] * o_s + g[:, :, 2, :, None] * o_w
        return o.astype(q.dtype)

    return sweep(block, T)


def mla(q_nope, q_rope, rows, w_uk, w_uv, q_start):
    T = q_nope.shape[1]
    ckv, kr = rows[..., :KV_LORA], rows[..., KV_LORA:]
    kpos = jnp.arange(rows.shape[1])
    q_lat = jnp.einsum('nthe,rhe->nthr', q_nope, w_uk)
    scale = (NOPE_DIM + ROPE_DIM) ** -0.5

    def block(start, qb):
        ql = lax.dynamic_slice_in_dim(q_lat, start, qb, axis=1)
        qr = lax.dynamic_slice_in_dim(q_rope, start, qb, axis=1)
        qpos = q_start + start + jnp.arange(qb)
        s = (ein('nqhr,nsr->nqhs', ql, ckv) + ein('nqhe,nse->nqhs', qr, kr)) * scale
        p = masked_softmax(s, (kpos[None, :] <= qpos[:, None])[:, None, :])
        return ein('nqhs,nsr->nqhr', p.astype(ckv.dtype), ckv).astype(q_lat.dtype)

    o_lat = sweep(block, T)
    return jnp.einsum('nthr,rhe->nthe', o_lat, w_uv)


def moba(q, k_full, fetch, q_start, table):
    N, T = q.shape[:2]
    nblk = k_full.shape[1] // MOBA_BLOCK
    n_top = min(MOBA_TOP, nblk)
    head_kv = jnp.arange(MOBA_HEADS) // (MOBA_HEADS // MOBA_KV_HEADS)
    kbar = k_full[:, :nblk * MOBA_BLOCK].astype(jnp.float32).reshape(
        N, nblk, MOBA_BLOCK, MOBA_KV_HEADS, HEAD_DIM).mean(2)[:, :, head_kv]
    blk = jnp.arange(nblk)
    off = jnp.arange(MOBA_BLOCK)
    bias_heads = (NSA_HEADS + jnp.arange(MOBA_HEADS))[None, None, :, None]
    scale = HEAD_DIM ** -0.5

    def block(start, qb):
        qblk = lax.dynamic_slice_in_dim(q, start, qb, axis=1)
        qpos = q_start + start + jnp.arange(qb)
        cur = qpos // MOBA_BLOCK
        own = jnp.broadcast_to((cur[:, None] * MOBA_BLOCK + off)[None, :, None, :], (N, qb, MOBA_HEADS, MOBA_BLOCK))
        pos_parts = [own]
        valid_parts = [own <= qpos[None, :, None, None]]
        if n_top > 0:
            gs = ein('nqhd,nbhd->nqhb', qblk, kbar)
            gs = jnp.where((blk[None, :] < cur[:, None])[None, :, None, :], gs, -jnp.inf)
            _, sel = lax.top_k(gs, n_top)
            pos_parts.append((sel[..., None] * MOBA_BLOCK + off).reshape(N, qb, MOBA_HEADS, n_top * MOBA_BLOCK))
            valid_parts.append(jnp.repeat(sel < cur[None, :, None, None], MOBA_BLOCK, axis=-1))
        kpos = jnp.concatenate(pos_parts, axis=-1)
        valid = jnp.concatenate(valid_parts, axis=-1)
        kv = fetch(kpos, head_kv[None, None, :, None])
        s = ein('nqhd,nqhmd->nqhm', qblk, kv[..., 0, :]) * scale + t5_bias(table, qpos[None, :, None, None] - kpos, bias_heads)
        p = masked_softmax(s, valid)
        return ein('nqhm,nqhmd->nqhd', p.astype(kv.dtype), kv[..., 1, :]).astype(q.dtype)

    return sweep(block, T)


def mixers(u, l, w, past):
    N, T, _ = u.shape
    proj = jnp.einsum('ntd,de->nte', u, w['w_in'][l])
    (sb_q, sb_kv, nsa_q, nsa_kv, nsa_g, mla_qa, mla_kva, mla_kr,
     moba_q, moba_kv, gate) = jnp.split(proj, np.cumsum(IN_SIZES)[:-1].tolist(), axis=-1)
    sb_q = sb_q.reshape(N, T, SB_HEADS, HEAD_DIM)
    sb_kv = sb_kv.reshape(N, T, SB_KV_HEADS, 2, HEAD_DIM)
    nsa_q = nsa_q.reshape(N, T, NSA_HEADS, HEAD_DIM)
    nsa_kv = nsa_kv.reshape(N, T, 3, 2, HEAD_DIM)
    nsa_rows = nsa_kv[:, :, :2]
    win_rows = nsa_kv[:, :, 2]
    nsa_g = jax.nn.sigmoid(nsa_g.reshape(N, T, 3, NSA_HEADS))
    moba_q = moba_q.reshape(N, T, MOBA_HEADS, HEAD_DIM)
    moba_kv = moba_kv.reshape(N, T, MOBA_KV_HEADS, 2, HEAD_DIM)
    gate = jax.nn.sigmoid(gate.reshape(N, T, N_BRANCH, D_MODEL))

    if past is None:
        q_start = 0
    else:
        (sb_pool, nsa_pool, win_state, mla_pool, moba_pool), page_table = past
        q_start = page_table.shape[1] * PAGE_SIZE
    pos = q_start + jnp.arange(T)

    c_kv = rms_norm(mla_kva, w['mla_kv_norm'][l])
    mla_rows = jnp.concatenate([c_kv, rope(mla_kr, pos)], axis=-1)
    q_mla = jnp.einsum('ntr,rhe->nthe', rms_norm(mla_qa, w['mla_q_norm'][l]), w['mla_w_uq'][l])
    q_nope, q_rope = q_mla[..., :NOPE_DIM], rope(q_mla[..., NOPE_DIM:], pos)

    if past is None:
        sb_full, cmp_full, mla_full = sb_kv, nsa_rows[:, :, 0], mla_rows
        moba_k_full = moba_kv[..., 0, :]
        sel_fetch = lambda p: dense_fetch(nsa_rows, p, 1)
        moba_fetch = lambda p, g: dense_fetch(moba_kv, p, g)
        win_pad = jnp.pad(win_rows, ((0, 0), (WINDOW, 0), (0, 0), (0, 0)))

        def window_rows(start, qb):
            return (lax.dynamic_slice_in_dim(win_pad, start, WINDOW + qb, axis=1),
                    start - WINDOW + jnp.arange(WINDOW + qb))

        new_win = win_rows[:, T - min(WINDOW, T):]
    else:
        sb_full = jnp.concatenate([flat_pages(sb_pool[l, page_table]), sb_kv], axis=1)
        cmp_full = jnp.concatenate([flat_pages(nsa_pool[l, page_table, :, 0]), nsa_rows[:, :, 0]], axis=1)
        mla_full = jnp.concatenate([flat_pages(mla_pool[l, page_table]), mla_rows], axis=1)
        moba_k_full = jnp.concatenate([flat_pages(moba_pool[l, page_table, :, :, 0]), moba_kv[..., 0, :]], axis=1)
        sel_fetch = lambda p: paged_fetch(nsa_pool, l, page_table, nsa_rows, p, 1)
        moba_fetch = lambda p, g: paged_fetch(moba_pool, l, page_table, moba_kv, p, g)
        win_buf = jnp.concatenate([win_state[l], win_rows], axis=1)
        win_pos = q_start - win_state.shape[2] + jnp.arange(win_buf.shape[1])
        window_rows = lambda start, qb: (win_buf, win_pos)
        new_win = win_buf[:, T:]

    o_sb = stick_breaking(sb_q, sb_full, q_start)
    cmp_tok = nsa_compress(cmp_full, w['nsa_cmp_pe'][l], w['nsa_cmp_w1'][l], w['nsa_cmp_w2'][l])
    o_nsa = nsa(nsa_q, nsa_g, cmp_tok, sel_fetch, window_rows, q_start, w['rel_bias'])
    o_mla = mla(q_nope, q_rope, mla_full, w['mla_w_uk'][l], w['mla_w_uv'][l], q_start)
    o_moba = moba(moba_q, moba_k_full, moba_fetch, q_start, w['rel_bias'])

    branches = jnp.stack([o.reshape(N, T, BR_WIDTH) for o in (o_sb, o_nsa, o_mla, o_moba)], axis=2)
    y = jnp.einsum('ntgc,gcd->ntgd', branches, w['w_br'][l])
    merged = jnp.sum(gate * y, axis=2)
    out = merged @ w['w_o'][l]
    return out, (sb_kv, nsa_rows, new_win, mla_rows, moba_kv)


def run_trunk(x, c, past, w):
    x = layer_norm(x, w['ln_in_g'], w['ln_in_b'])
    c_act = jax.nn.silu(c)
    new = ([], [], [], [], [])
    for l in range(DEPTH):
        mod = (c_act @ w['w_ada'][l] + w['b_ada'][l]).reshape(c.shape[0], 1, 6, D_MODEL)
        shift_m, scale_m, gate_m = mod[:, :, 0], mod[:, :, 1], mod[:, :, 2]
        shift_f, scale_f, gate_f = mod[:, :, 3], mod[:, :, 4], mod[:, :, 5]
        u = x * (1 + scale_m) + shift_m
        mix, rows = mixers(u, l, w, past)
        x = layer_norm(ALPHA * x + gate_m * mix, w['ln1_g'][l], w['ln1_b'][l])
        u = x * (1 + scale_f) + shift_f
        f = (jax.nn.silu(u @ w['w_ff1'][l]) * (u @ w['w_ff3'][l])) @ w['w_ff2'][l]
        x = layer_norm(ALPHA * x + gate_f * f, w['ln2_g'][l], w['ln2_b'][l])
        for lst, r in zip(new, rows):
            lst.append(r)
    return x, [jnp.stack(lst, axis=0) for lst in new]


def setup_inputs(seed: int = 0) -> dict:
    key = jax.random.key(seed)
    ks = iter(jax.random.split(key, 40))
    f32 = jnp.float32

    def nrm(shape, scale=1.0):
        return jax.random.normal(next(ks), shape, f32) * scale

    n_pages = PAST_LEN // PAGE_SIZE
    n_used = DEC_BATCH * n_pages
    n_pool = n_used + (n_used + 3) // 4
    win_len = min(WINDOW, PAST_LEN)
    D = D_MODEL
    inp = {}
    inp['x_prompt'] = nrm((BATCH, SEQ, D))
    inp['x_sample'] = nrm((DEC_BATCH, DEC_SEQ, D))
    inp['cache_sb_kv'] = nrm((DEPTH, n_pool, PAGE_SIZE, SB_KV_HEADS, 2, HEAD_DIM))
    inp['cache_nsa_kv'] = nrm((DEPTH, n_pool, PAGE_SIZE, 2, 2, HEAD_DIM))
    inp['state_nsa_win'] = nrm((DEPTH, DEC_BATCH, win_len, 2, HEAD_DIM))
    inp['cache_mla'] = nrm((DEPTH, n_pool, PAGE_SIZE, KV_LORA + ROPE_DIM))
    inp['cache_moba_kv'] = nrm((DEPTH, n_pool, PAGE_SIZE, MOBA_KV_HEADS, 2, HEAD_DIM))
    inp['page_table'] = jax.random.permutation(next(ks), n_pool)[:n_used].reshape(DEC_BATCH, n_pages).astype(jnp.int32)
    inp['c_prompt'] = nrm((BATCH, D))
    inp['c_sample'] = nrm((DEC_BATCH, D))
    inp['rel_bias'] = nrm((NUM_BUCKETS, N_BIAS_HEADS), 0.5)
    inp['ln_in_g'] = 1.0 + nrm((D,), 0.02)
    inp['ln_in_b'] = nrm((D,), 0.02)
    inp['w_ada'] = nrm((DEPTH, D, 6 * D), 0.5 * D ** -0.5)
    inp['b_ada'] = nrm((DEPTH, 6 * D), 0.02)
    inp['w_in'] = nrm((DEPTH, D, N_IN), D ** -0.5)
    inp['nsa_cmp_pe'] = nrm((DEPTH, 2, CMP_LEN, HEAD_DIM), 0.1)
    inp['nsa_cmp_w1'] = nrm((DEPTH, 2, CMP_LEN * HEAD_DIM, CMP_HIDDEN), (CMP_LEN * HEAD_DIM) ** -0.5)
    inp['nsa_cmp_w2'] = nrm((DEPTH, 2, CMP_HIDDEN, HEAD_DIM), CMP_HIDDEN ** -0.5)
    inp['mla_q_norm'] = 1.0 + nrm((DEPTH, Q_LORA), 0.02)
    inp['mla_kv_norm'] = 1.0 + nrm((DEPTH, KV_LORA), 0.02)
    inp['mla_w_uq'] = nrm((DEPTH, Q_LORA, MLA_HEADS, NOPE_DIM + ROPE_DIM), Q_LORA ** -0.5)
    inp['mla_w_uk'] = nrm((DEPTH, KV_LORA, MLA_HEADS, NOPE_DIM), KV_LORA ** -0.5)
    inp['mla_w_uv'] = nrm((DEPTH, KV_LORA, MLA_HEADS, V_DIM), KV_LORA ** -0.5)
    inp['w_br'] = nrm((DEPTH, N_BRANCH, BR_WIDTH, D), BETA_INIT * BR_WIDTH ** -0.5)
    inp['w_o'] = nrm((DEPTH, D, D), BETA_INIT * D ** -0.5)
    inp['ln1_g'] = 1.0 + nrm((DEPTH, D), 0.02)
    inp['ln1_b'] = nrm((DEPTH, D), 0.02)
    inp['w_ff1'] = nrm((DEPTH, D, D_FF), D ** -0.5)
    inp['w_ff3'] = nrm((DEPTH, D, D_FF), D ** -0.5)
    inp['w_ff2'] = nrm((DEPTH, D_FF, D), BETA_INIT * D_FF ** -0.5)
    inp['ln2_g'] = 1.0 + nrm((DEPTH, D), 0.02)
    inp['ln2_b'] = nrm((DEPTH, D), 0.02)
    return inp


def reference(x_prompt, x_sample, cache_sb_kv, cache_nsa_kv, state_nsa_win, cache_mla, cache_moba_kv,
              page_table, c_prompt, c_sample, rel_bias, ln_in_g, ln_in_b, w_ada, b_ada, w_in,
              nsa_cmp_pe, nsa_cmp_w1, nsa_cmp_w2, mla_q_norm, mla_kv_norm, mla_w_uq, mla_w_uk, mla_w_uv,
              w_br, w_o, ln1_g, ln1_b, w_ff1, w_ff3, w_ff2, ln2_g, ln2_b):
    w = dict(rel_bias=rel_bias, ln_in_g=ln_in_g, ln_in_b=ln_in_b, w_ada=w_ada, b_ada=b_ada, w_in=w_in,
             nsa_cmp_pe=nsa_cmp_pe, nsa_cmp_w1=nsa_cmp_w1, nsa_cmp_w2=nsa_cmp_w2,
             mla_q_norm=mla_q_norm, mla_kv_norm=mla_kv_norm, mla_w_uq=mla_w_uq, mla_w_uk=mla_w_uk,
             mla_w_uv=mla_w_uv, w_br=w_br, w_o=w_o, ln1_g=ln1_g, ln1_b=ln1_b,
             w_ff1=w_ff1, w_ff3=w_ff3, w_ff2=w_ff2, ln2_g=ln2_g, ln2_b=ln2_b)
    y_prompt, st_p = run_trunk(x_prompt, c_prompt, None, w)
    pools = (cache_sb_kv, cache_nsa_kv, state_nsa_win, cache_mla, cache_moba_kv)
    y_sample, st_s = run_trunk(x_sample, c_sample, (pools, page_table), w)
    sb_p, nsa_p, win_p, mla_p, moba_p = st_p
    sb_s, nsa_s, win_s, mla_s, moba_s = st_s
    return (y_prompt, y_sample, sb_p, sb_s, nsa_p, nsa_s, win_p, win_s, mla_p, mla_s, moba_p, moba_s)
```

```python
import functools
import math

import numpy as np
import jax
import jax.numpy as jnp
from jax import lax
from jax.experimental import pallas as pl
from jax.experimental.pallas import tpu as pltpu

F32 = jnp.float32
BF16 = jnp.bfloat16

D_MODEL = 1024
PAGE = 128
HD = 64
SLAB = 2 * HD
N_HEADS = 4
CMP_STRIDE = 16
CMP_LEN = 32
CMP_HIDDEN = 128
SEL_BLOCK = 64
SEL_TOP = 16
WINDOW = 512
Q_LORA = 256
KV_LORA = 128
NOPE = 64
ROPE = 32
ROPE_THETA = 10000.0
MOBA_BLOCK = 256
MOBA_TOP = 3
NUM_BUCKETS = 32
MAX_DISTANCE = 128
DEPTH = 2
D_FF = 2816
ALPHA = (2 * DEPTH) ** 0.25
N_IN_MIX = 2092
TILE = 256
NEG = -1e30
VMEM_LIMIT = 56 * 1024 * 1024

C_SBQ, C_SBKV, C_NSAQ, C_NSAKV, C_NSAG, C_QA, C_KVA, C_KR, C_KRS, C_MOBAQ, C_MOBAKV, C_END = (
    0, 512, 768, 1280, 1664, 1792, 2048, 2176, 2304, 2432, 2944, 3200)


def _cparams(sem, vmem=VMEM_LIMIT):
    return pltpu.CompilerParams(dimension_semantics=sem, vmem_limit_bytes=vmem)


def _dot(a, b):
    return jnp.dot(a, b, preferred_element_type=F32)


def _dot_nt(a, b):
    return lax.dot_general(a, b, (((1,), (1,)), ((), ())), preferred_element_type=F32)


def _split_dot(x, w):
    hi = x.astype(BF16)
    lo = (x - hi.astype(F32)).astype(BF16)
    return _dot(hi, w) + _dot(lo, w)


def _layer_norm(x, g, b):
    mu = jnp.mean(x, axis=-1, keepdims=True)
    xc = x - mu
    var = jnp.mean(xc * xc, axis=-1, keepdims=True)
    return xc * lax.rsqrt(var + 1e-5) * g + b


def _sigmoid(x):
    return 1.0 / (1.0 + jnp.exp(-x))


def _pad_heads(w):
    k = w.shape[0]
    return jnp.pad(w.reshape(k, N_HEADS, HD), ((0, 0), (0, 0), (0, HD))).reshape(k, N_HEADS * SLAB)


def _pad_cols(w, n):
    return jnp.pad(w, ((0, 0), (0, n - w.shape[1])))


def _pack_layer(l, w_in, mla_w_uq, mla_w_uk, mla_w_uv, w_br, nsa_cmp_pe, nsa_cmp_w1, nsa_cmp_w2):
    wi = w_in[l]
    o = np.cumsum([0, 256, 256, 256, 384, 12, 256, 128, 32, 256, 256]).tolist()
    sb_q, sb_kv, nsa_q, nsa_kv, nsa_g, qa, kva, kr, moba_q, moba_kv = [wi[:, o[i]:o[i + 1]] for i in range(10)]
    kr_sw = jnp.concatenate([kr[:, ROPE // 2:], kr[:, :ROPE // 2]], axis=1)
    w_proj = jnp.concatenate([
        _pad_heads(sb_q), sb_kv, _pad_heads(nsa_q), nsa_kv, _pad_cols(nsa_g, 128), qa, kva,
        _pad_cols(kr, 128), _pad_cols(kr_sw, 128), _pad_heads(moba_q), moba_kv], axis=1).astype(BF16)
    w_gate = wi[:, N_IN_MIX:].astype(BF16)
    uq = mla_w_uq[l]
    uq_nope = uq[:, :, :NOPE].reshape(Q_LORA, N_HEADS * NOPE)
    uq_rope = uq[:, :, NOPE:]
    uq_rope_sw = jnp.concatenate([uq_rope[..., ROPE // 2:], uq_rope[..., :ROPE // 2]], axis=-1)
    place = lambda r: jnp.pad(r, ((0, 0), (0, 0), (0, 128 - ROPE))).reshape(Q_LORA, N_HEADS * 128)
    w_q3 = jnp.concatenate([uq_nope, place(uq_rope), place(uq_rope_sw)], axis=1).astype(BF16)
    uk = mla_w_uk[l]
    w_uk = jnp.zeros((N_HEADS * NOPE, N_HEADS * KV_LORA), F32)
    for h in range(N_HEADS):
        w_uk = w_uk.at[h * NOPE:(h + 1) * NOPE, h * KV_LORA:(h + 1) * KV_LORA].set(uk[:, h, :].T)
    w_uk = w_uk.astype(BF16)
    uv = mla_w_uv[l]
    w_uv = jnp.pad(jnp.transpose(uv, (1, 0, 2)), ((0, 0), (0, 0), (HD, 0))).astype(BF16)
    wb = w_br[l].reshape(4, N_HEADS, HD, D_MODEL)
    w_brp = jnp.pad(wb, ((0, 0), (0, 0), (HD, 0), (0, 0))).reshape(4, N_HEADS * SLAB, D_MODEL).astype(BF16)
    w1 = nsa_cmp_w1[l].reshape(2, 2, CMP_STRIDE, HD, CMP_HIDDEN)
    halves = []
    for half in range(2):
        wh = jnp.zeros((CMP_STRIDE, 4, HD, 2, CMP_HIDDEN), F32)
        for c in range(2):
            wh = wh.at[:, c, :, c, :].set(w1[c, half])
        halves.append(wh.reshape(CMP_STRIDE * 4 * HD, 2 * CMP_HIDDEN))
    w_c1 = jnp.concatenate(halves, axis=1).astype(BF16)
    pe = nsa_cmp_pe[l].reshape(2, 2, CMP_STRIDE, HD)
    pe_rows = []
    for half in range(2):
        p = jnp.zeros((CMP_STRIDE, 4, HD), F32)
        for c in range(2):
            p = p.at[:, c, :].set(pe[c, half])
        pe_rows.append(p.reshape(1, CMP_STRIDE * 4 * HD))
    pe_rows = jnp.concatenate(pe_rows + [jnp.zeros((6, CMP_STRIDE * 4 * HD), F32)], axis=0)
    w2 = nsa_cmp_w2[l]
    w_c2 = jnp.zeros((2 * CMP_HIDDEN, 2 * HD), F32)
    for c in range(2):
        w_c2 = w_c2.at[c * CMP_HIDDEN:(c + 1) * CMP_HIDDEN, c * HD:(c + 1) * HD].set(w2[c])
    w_c2 = w_c2.astype(BF16)
    return dict(w_proj=w_proj, w_gate=w_gate, w_q3=w_q3, w_uk=w_uk, w_uv=w_uv, w_brp=w_brp,
                w_c1=w_c1, pe_rows=pe_rows, w_c2=w_c2)


def _rope_freq():
    half = ROPE // 2
    inv = ROPE_THETA ** (-np.arange(half, dtype=np.float32) / half)
    f = np.zeros((1, 128), np.float32)
    f[0, :half] = -inv
    f[0, half:ROPE] = inv
    return jnp.asarray(f)


def _ada_kernel(c_ref, w_ref, b_ref, o_ref):
    c = c_ref[...]
    act = (c * _sigmoid(c)).astype(BF16)
    o_ref[...] = _dot(act, w_ref[...].astype(BF16)) + b_ref[...]


def _ada(c, w, b):
    n = c.shape[0]
    npad = -(-n // 8) * 8
    cp = jnp.pad(c, ((0, npad - n), (0, 0)))
    cols = 6 * D_MODEL
    tn = 1536
    out = pl.pallas_call(
        _ada_kernel,
        grid=(cols // tn,),
        in_specs=[pl.BlockSpec((npad, D_MODEL), lambda j: (0, 0)),
                  pl.BlockSpec((D_MODEL, tn), lambda j: (0, j)),
                  pl.BlockSpec((1, tn), lambda j: (0, j))],
        out_specs=pl.BlockSpec((npad, tn), lambda j: (0, j)),
        out_shape=jax.ShapeDtypeStruct((npad, cols), F32),
        compiler_params=_cparams(("arbitrary",)),
        name="ada",
    )(cp, w, b.reshape(1, cols))
    return out[:n]


def _proj_kernel(x_ref, mod_ref, lng_ref, lnb_ref, w_ref, qn_ref, kvn_ref, wq3_ref, wuk_ref, freq_ref, *outs,
                 apply_ln, pos0, period, emit_kbar):
    outs = list(outs)
    (sbq_o, sbkv_o, sbkvb_o, nsaq_o, nsakv_o, nsawin_o, nsakvb_o, nsag_o, mlar_o, mlarb_o, mlaq_o,
     mobaq_o, mobakv_o, mobakvb_o) = outs[:14]
    rest = outs[14:]
    tm = x_ref.shape[1]
    x = x_ref[0]
    if apply_ln:
        x = _layer_norm(x, lng_ref[...], lnb_ref[...])
        rest.pop(0)[0] = x
    mod = mod_ref[0]
    u = (x * (1.0 + mod[:, D_MODEL:2 * D_MODEL]) + mod[:, 0:D_MODEL]).astype(BF16)

    def grp(a, b):
        return _dot(u, w_ref[:, a:b])

    qscale = HD ** -0.5
    sbq_o[0] = (grp(C_SBQ, C_SBKV) * qscale).astype(BF16)
    kv = grp(C_SBKV, C_NSAQ)
    sbkv_o[0] = kv
    sbkvb_o[0] = kv.astype(BF16)
    nsaq_o[0] = (grp(C_NSAQ, C_NSAKV) * qscale).astype(BF16)
    kv = grp(C_NSAKV, C_NSAG)
    nsakv_o[0] = kv[:, 0:256]
    nsawin_o[0] = kv[:, 256:384]
    nsakvb_o[0] = kv.astype(BF16)
    nsag_o[0] = _sigmoid(grp(C_NSAG, C_QA))
    mobaq_o[0] = (grp(C_MOBAQ, C_MOBAKV) * qscale).astype(BF16)
    kv = grp(C_MOBAKV, C_END)
    mobakv_o[0] = kv
    mobakvb_o[0] = kv.astype(BF16)
    if emit_kbar:
        rest.pop(0)[0, 0] = jnp.mean(kv, axis=0, keepdims=True)

    row = lax.broadcasted_iota(jnp.int32, (tm, 1), 0) + pl.program_id(1) * tm
    pos = (pos0 + (row & (period - 1))).astype(F32)
    ang = pos * freq_ref[...]
    cs, sn = jnp.cos(ang), jnp.sin(ang)

    kva = grp(C_KVA, C_KR)
    ckv = kva * lax.rsqrt(jnp.mean(kva * kva, axis=-1, keepdims=True) + 1e-6) * kvn_ref[...]
    kr = grp(C_KR, C_KRS) * cs + grp(C_KRS, C_MOBAQ) * sn
    mlar_o[0, :, 0:KV_LORA] = ckv
    mlar_o[0, :, KV_LORA:KV_LORA + ROPE] = kr[:, 0:ROPE]
    mlarb_o[0] = jnp.concatenate([ckv, kr], axis=1).astype(BF16)

    qa = grp(C_QA, C_KVA)
    qan = (qa * lax.rsqrt(jnp.mean(qa * qa, axis=-1, keepdims=True) + 1e-6) * qn_ref[...]).astype(BF16)
    q3 = _dot(qan, wq3_ref[...])
    qlat = _dot(q3[:, 0:256].astype(BF16), wuk_ref[...])
    mscale = (NOPE + ROPE) ** -0.5
    for h in range(N_HEADS):
        a = q3[:, 256 + 128 * h:384 + 128 * h]
        b = q3[:, 768 + 128 * h:896 + 128 * h]
        mlaq_o[0, :, 256 * h:256 * h + 128] = (qlat[:, 128 * h:128 * h + 128] * mscale).astype(BF16)
        mlaq_o[0, :, 256 * h + 128:256 * h + 256] = ((a * cs + b * sn) * mscale).astype(BF16)


def _proj(x, mod, lng, lnb, pk, qn, kvn, *, apply_ln, pos0, period, mod_per_row, emit_kbar):
    nb, r, _ = x.shape
    tm = min(TILE, r)
    grid = (nb, r // tm)
    row = lambda c, dt: (pl.BlockSpec((1, tm, c), lambda n, i: (n, i, 0)), jax.ShapeDtypeStruct((nb, r, c), dt))
    outs = [row(512, BF16), row(256, F32), row(256, BF16), row(512, BF16), row(256, F32), row(128, F32),
            row(384, BF16), row(128, F32), row(KV_LORA + ROPE, F32), row(256, BF16), row(1024, BF16),
            row(512, BF16), row(256, F32), row(256, BF16)]
    if apply_ln:
        outs.append(row(D_MODEL, F32))
    if emit_kbar:
        outs.append((pl.BlockSpec((1, 1, 1, 256), lambda n, i: (n, i, 0, 0)),
                     jax.ShapeDtypeStruct((nb, r // tm, 1, 256), F32)))
    full = lambda a: pl.BlockSpec(a.shape, lambda n, i: (0,) * a.ndim)
    if mod_per_row:
        mod_spec = pl.BlockSpec((1, tm, 6 * D_MODEL), lambda n, i: (n, i, 0))
    else:
        mod_spec = pl.BlockSpec((1, 1, 6 * D_MODEL), lambda n, i: (n, 0, 0))
    freq = _rope_freq()
    args = [x, mod, lng.reshape(1, -1), lnb.reshape(1, -1), pk["w_proj"], qn.reshape(1, -1), kvn.reshape(1, -1),
            pk["w_q3"], pk["w_uk"], freq]
    in_specs = [pl.BlockSpec((1, tm, D_MODEL), lambda n, i: (n, i, 0)), mod_spec] + [full(a) for a in args[2:]]
    res = pl.pallas_call(
        functools.partial(_proj_kernel, apply_ln=apply_ln, pos0=pos0, period=period, emit_kbar=emit_kbar),
        grid=grid, in_specs=in_specs,
        out_specs=[o[0] for o in outs], out_shape=[o[1] for o in outs],
        compiler_params=_cparams(("parallel", "parallel")),
        name="proj",
    )(*args)
    return res


def _merge_kernel(x_ref, mod_ref, o0, o1, o2, o3, wg_ref, wbr_ref, wo_ref, g_ref, b_ref, out_ref):
    x = x_ref[0]
    mod = mod_ref[0]
    u = (x * (1.0 + mod[:, D_MODEL:2 * D_MODEL]) + mod[:, 0:D_MODEL]).astype(BF16)
    merged = None
    for bi, o in enumerate((o0, o1, o2, o3)):
        gate = _sigmoid(_dot(u, wg_ref[:, bi * D_MODEL:(bi + 1) * D_MODEL]))
        y = gate * _dot(o[0].astype(BF16), wbr_ref[bi])
        merged = y if merged is None else merged + y
    mix = _dot(merged.astype(BF16), wo_ref[...])
    out_ref[0] = _layer_norm(ALPHA * x + mod[:, 2 * D_MODEL:3 * D_MODEL] * mix, g_ref[...], b_ref[...])


def _ffn_kernel(x_ref, mod_ref, w1_ref, w3_ref, w2_ref, g_ref, b_ref, out_ref):
    x = x_ref[0]
    mod = mod_ref[0]
    u = (x * (1.0 + mod[:, 4 * D_MODEL:5 * D_MODEL]) + mod[:, 3 * D_MODEL:4 * D_MODEL]).astype(BF16)
    a = _dot(u, w1_ref[...])
    h = (a * _sigmoid(a) * _dot(u, w3_ref[...])).astype(BF16)
    f = _dot(h, w2_ref[...])
    out_ref[0] = _layer_norm(ALPHA * x + mod[:, 5 * D_MODEL:6 * D_MODEL] * f, g_ref[...], b_ref[...])


def _row_call(kernel, name, x, mod, mod_per_row, row_ins, consts):
    nb, r, _ = x.shape
    tm = min(TILE, r)
    if mod_per_row:
        mod_spec = pl.BlockSpec((1, tm, 6 * D_MODEL), lambda n, i: (n, i, 0))
    else:
        mod_spec = pl.BlockSpec((1, 1, 6 * D_MODEL), lambda n, i: (n, 0, 0))
    rspec = lambda a: pl.BlockSpec((1, tm, a.shape[-1]), lambda n, i: (n, i, 0))
    full = lambda a: pl.BlockSpec(a.shape, lambda n, i: (0,) * a.ndim, pipeline_mode=pl.Buffered(1))
    return pl.pallas_call(
        kernel, grid=(nb, r // tm),
        in_specs=[rspec(x), mod_spec] + [rspec(a) for a in row_ins] + [full(a) for a in consts],
        out_specs=rspec(x), out_shape=jax.ShapeDtypeStruct(x.shape, F32),
        compiler_params=_cparams(("parallel", "parallel")),
        name=name,
    )(x, mod, *row_ins, *consts)


def _neg_softplus(z):
    return -(jnp.maximum(z, 0.0) + jnp.log(1.0 + jnp.exp(-jnp.abs(z))))


def _sb_tile(z, kvt, tri, carry, mask):
    lr = _neg_softplus(z)
    if mask is not None:
        lr = jnp.where(mask, lr, 0.0)
    tail = _split_dot(lr, tri)
    w = jnp.exp(z + lr + tail + carry)
    if mask is not None:
        w = jnp.where(mask, w, 0.0)
    contrib = _dot(w.astype(BF16), kvt)
    return contrib, carry + tail[:, 0:1] + lr[:, 0:1]


def _sb_prompt_kernel(q_ref, kv_ref, tri_ref, o_ref, acc_ref, carry_ref):
    tq = q_ref.shape[1]
    i = pl.program_id(1)
    tri = tri_ref[...]
    rowq = lax.broadcasted_iota(jnp.int32, (2 * tq, tq), 0) & (tq - 1)
    col = lax.broadcasted_iota(jnp.int32, (2 * tq, tq), 1)
    for kvh in range(2):
        q2 = jnp.concatenate([q_ref[0, :, SLAB * (2 * kvh):SLAB * (2 * kvh + 1)],
                              q_ref[0, :, SLAB * (2 * kvh + 1):SLAB * (2 * kvh + 2)]], axis=0)

        def kv_tile(j):
            return kv_ref[0, pl.ds(pl.multiple_of(j * tq, tq), tq), SLAB * kvh:SLAB * (kvh + 1)]

        kvt = kv_tile(i)
        contrib, carry = _sb_tile(_dot_nt(q2, kvt), kvt, tri, jnp.zeros((2 * tq, 1), F32), col < rowq)
        acc_ref[...] = contrib
        carry_ref[...] = carry

        def body(k, _):
            kvt = kv_tile(i - 1 - k)
            contrib, carry = _sb_tile(_dot_nt(q2, kvt), kvt, tri, carry_ref[...], None)
            acc_ref[...] += contrib
            carry_ref[...] = carry
            return 0

        lax.fori_loop(0, i, body, 0)
        acc = acc_ref[...].astype(BF16)
        o_ref[0, :, SLAB * (2 * kvh):SLAB * (2 * kvh + 1)] = acc[0:tq]
        o_ref[0, :, SLAB * (2 * kvh + 1):SLAB * (2 * kvh + 2)] = acc[tq:2 * tq]


def _tri(n):
    r = np.arange(n)
    return jnp.asarray((r[:, None] > r[None, :]).astype(np.float32), dtype=BF16)


def _sb_prompt(q, kvb):
    nb, t, _ = q.shape
    tq = TILE
    return pl.pallas_call(
        _sb_prompt_kernel, grid=(nb, t // tq),
        in_specs=[pl.BlockSpec((1, tq, 512), lambda n, i: (n, i, 0)),
                  pl.BlockSpec((1, t, 256), lambda n, i: (n, 0, 0)),
                  pl.BlockSpec((tq, tq), lambda n, i: (0, 0))],
        out_specs=pl.BlockSpec((1, tq, 512), lambda n, i: (n, i, 0)),
        out_shape=jax.ShapeDtypeStruct((nb, t, 512), BF16),
        scratch_shapes=[pltpu.VMEM((2 * tq, SLAB), F32), pltpu.VMEM((2 * tq, 1), F32)],
        compiler_params=_cparams(("parallel", "arbitrary")),
        name="sb_prompt",
    )(q, kvb, _tri(tq))


def _softmax_step(s, vt, m_ref, l_ref, acc_ref, first):
    if first:
        m_new = jnp.max(s, axis=1, keepdims=True)
        p = jnp.exp(s - m_new)
        l_ref[...] = jnp.sum(p, axis=1, keepdims=True)
        acc_ref[...] = _dot(p.astype(BF16), vt)
    else:
        m_old = m_ref[...]
        m_new = jnp.maximum(m_old, jnp.max(s, axis=1, keepdims=True))
        a = jnp.exp(m_old - m_new)
        p = jnp.exp(s - m_new)
        l_ref[...] = a * l_ref[...] + jnp.sum(p, axis=1, keepdims=True)
        acc_ref[...] = a * acc_ref[...] + _dot(p.astype(BF16), vt)
    m_ref[...] = m_new


def _mla_prompt_kernel(q_ref, rows_ref, wuv_ref, o_ref, m_ref, l_ref, acc_ref):
    tq = q_ref.shape[1]
    i = pl.program_id(1)
    q4 = jnp.concatenate([q_ref[0, :, 256 * h:256 * (h + 1)] for h in range(N_HEADS)], axis=0)
    rowq = lax.broadcasted_iota(jnp.int32, (N_HEADS * tq, tq), 0) & (tq - 1)
    col = lax.broadcasted_iota(jnp.int32, (N_HEADS * tq, tq), 1)

    def rows_tile(j):
        return rows_ref[0, pl.ds(pl.multiple_of(j * tq, tq), tq), :]

    rt = rows_tile(i)
    s = jnp.where(col <= rowq, _dot_nt(q4, rt), NEG)
    _softmax_step(s, rt, m_ref, l_ref, acc_ref, True)

    def body(k, _):
        rt = rows_tile(i - 1 - k)
        _softmax_step(_dot_nt(q4, rt), rt, m_ref, l_ref, acc_ref, False)
        return 0

    lax.fori_loop(0, i, body, 0)
    olat = (acc_ref[:, 0:KV_LORA] / jnp.maximum(l_ref[...], 1e-30)).astype(BF16)
    for h in range(N_HEADS):
        o_ref[0, :, SLAB * h:SLAB * (h + 1)] = _dot(olat[h * tq:(h + 1) * tq], wuv_ref[h]).astype(BF16)


def _mla_prompt(q, rowsb, w_uv):
    nb, t, _ = q.shape
    tq = TILE
    return pl.pallas_call(
        _mla_prompt_kernel, grid=(nb, t // tq),
        in_specs=[pl.BlockSpec((1, tq, 1024), lambda n, i: (n, i, 0)),
                  pl.BlockSpec((1, t, 256), lambda n, i: (n, 0, 0)),
                  pl.BlockSpec((N_HEADS, 128, 128), lambda n, i: (0, 0, 0))],
        out_specs=pl.BlockSpec((1, tq, 512), lambda n, i: (n, i, 0)),
        out_shape=jax.ShapeDtypeStruct((nb, t, 512), BF16),
        scratch_shapes=[pltpu.VMEM((N_HEADS * tq, 1), F32), pltpu.VMEM((N_HEADS * tq, 1), F32),
                        pltpu.VMEM((N_HEADS * tq, 256), F32)],
        compiler_params=_cparams(("parallel", "arbitrary")),
        name="mla_prompt",
    )(q, rowsb, w_uv)


def _t5_bucket(n):
    exact = NUM_BUCKETS // 2
    nf = jnp.maximum(n, 1).astype(F32)
    far = exact + (jnp.log(nf / exact) / math.log(MAX_DISTANCE / exact) * (NUM_BUCKETS - exact)).astype(jnp.int32)
    return jnp.where(n < exact, n, jnp.minimum(far, NUM_BUCKETS - 1))


def _bias_kernel(tab_ref, lut_ref, tile_ref):
    h = pl.program_id(0)
    t = tile_ref.shape[2]

    def lookup(n):
        bucket = _t5_bucket(jnp.maximum(n, 0))
        out = jnp.zeros(n.shape, F32)
        for b in range(NUM_BUCKETS):
            out = jnp.where(bucket == b, tab_ref[b, h], out)
        return out

    lut_ref[0] = lookup(lax.broadcasted_iota(jnp.int32, (8, 128), 1))
    rel = lax.broadcasted_iota(jnp.int32, (t, t), 0) - lax.broadcasted_iota(jnp.int32, (t, t), 1)
    tile_ref[0, 0] = lookup(rel)
    tile_ref[0, 1] = lookup(rel + t)


def _bias_tables(rel_bias):
    nh = rel_bias.shape[1]
    lut, tiles = pl.pallas_call(
        _bias_kernel, grid=(nh,),
        in_specs=[pl.BlockSpec(memory_space=pltpu.SMEM)],
        out_specs=[pl.BlockSpec((1, 8, 128), lambda h: (h, 0, 0)),
                   pl.BlockSpec((1, 2, TILE, TILE), lambda h: (h, 0, 0, 0))],
        out_shape=[jax.ShapeDtypeStruct((nh, 8, 128), F32), jax.ShapeDtypeStruct((nh, 2, TILE, TILE), F32)],
        compiler_params=_cparams(("arbitrary",)),
        name="bias_tables",
    )(rel_bias)
    return lut[:, 0, :], tiles


def _group_tiles(tiles, far, groups):
    h = tiles.shape[0]
    g = h // groups
    t = tiles.shape[2]
    bt = tiles.reshape(groups, g, 2, t, t).transpose(0, 2, 1, 3, 4).reshape(groups, 2, g * t, t)
    fr = jnp.broadcast_to(far.reshape(groups, g, 1, 1), (groups, g, t, 1)).reshape(groups, g * t, 1)
    return bt, fr


def _topk_ids(score, k):
    col = lax.broadcasted_iota(jnp.int32, score.shape, 1)
    ids = []
    for _ in range(k):
        m = jnp.max(score, axis=1, keepdims=True)
        first = jnp.min(jnp.where(score == m, col, 1 << 20), axis=1, keepdims=True)
        ids.append(jnp.where(m > 0.5 * NEG, first, -1))
        score = jnp.where(col == first, NEG, score)
    return ids


def _moba_prompt_kernel(q_ref, kv_ref, kbar_ref, bt_ref, far_ref, o_ref, m_ref, l_ref, acc_ref):
    tq = q_ref.shape[1]
    i = pl.program_id(1)
    rowq = lax.broadcasted_iota(jnp.int32, (2 * tq, tq), 0) & (tq - 1)
    col = lax.broadcasted_iota(jnp.int32, (2 * tq, tq), 1)
    bcol = lax.broadcasted_iota(jnp.int32, (2 * tq, kbar_ref.shape[1]), 1)
    for kvh in range(2):
        q2 = jnp.concatenate([q_ref[0, :, SLAB * (2 * kvh):SLAB * (2 * kvh + 1)],
                              q_ref[0, :, SLAB * (2 * kvh + 1):SLAB * (2 * kvh + 2)]], axis=0)
        kb = kbar_ref[0, :, SLAB * kvh:SLAB * (kvh + 1)]
        kb_hi = kb.astype(BF16)
        kb_lo = (kb - kb_hi.astype(F32)).astype(BF16)
        gs = _dot_nt(q2, kb_hi) + _dot_nt(q2, kb_lo)
        ids = _topk_ids(jnp.where(bcol < i, gs, NEG), MOBA_TOP)

        def kv_tile(j):
            return kv_ref[0, pl.ds(pl.multiple_of(j * tq, tq), tq), SLAB * kvh:SLAB * (kvh + 1)]

        kvt = kv_tile(i)
        s = jnp.where(col <= rowq, _dot_nt(q2, kvt) + bt_ref[kvh, 0], NEG)
        _softmax_step(s, kvt, m_ref, l_ref, acc_ref, True)

        def body(k, _):
            j = i - 1 - k
            kvt = kv_tile(j)
            bias = jnp.where(k == 0, bt_ref[kvh, 1], far_ref[kvh])
            chosen = (ids[0] == j) | (ids[1] == j) | (ids[2] == j)
            s = jnp.where(chosen, _dot_nt(q2, kvt) + bias, NEG)
            _softmax_step(s, kvt, m_ref, l_ref, acc_ref, False)
            return 0

        lax.fori_loop(0, i, body, 0)
        out = (acc_ref[...] / jnp.maximum(l_ref[...], 1e-30)).astype(BF16)
        o_ref[0, :, SLAB * (2 * kvh):SLAB * (2 * kvh + 1)] = out[0:tq]
        o_ref[0, :, SLAB * (2 * kvh + 1):SLAB * (2 * kvh + 2)] = out[tq:2 * tq]


def _moba_prompt(q, kvb, kbar, bt, far):
    nb, t, _ = q.shape
    tq = TILE
    assert tq == MOBA_BLOCK
    full = lambda a: pl.BlockSpec(a.shape, lambda n, i: (0,) * a.ndim)
    return pl.pallas_call(
        _moba_prompt_kernel, grid=(nb, t // tq),
        in_specs=[pl.BlockSpec((1, tq, 512), lambda n, i: (n, i, 0)),
                  pl.BlockSpec((1, t, 256), lambda n, i: (n, 0, 0)),
                  pl.BlockSpec((1,) + kbar.shape[1:], lambda n, i: (n, 0, 0)),
                  full(bt), full(far)],
        out_specs=pl.BlockSpec((1, tq, 512), lambda n, i: (n, i, 0)),
        out_shape=jax.ShapeDtypeStruct((nb, t, 512), BF16),
        scratch_shapes=[pltpu.VMEM((2 * tq, 1), F32), pltpu.VMEM((2 * tq, 1), F32),
                        pltpu.VMEM((2 * tq, SLAB), F32)],
        compiler_params=_cparams(("parallel", "arbitrary")),
        name="moba_prompt",
    )(q, kvb, kbar, bt, far)


def _cmp_a_kernel(x_ref, w_ref, o_ref):
    o_ref[0] = _dot(x_ref[0].astype(BF16), w_ref[...])


def _cmp_a(x, w_c1):
    nb, j1, c = x.shape
    tj = min(TILE, j1)
    return pl.pallas_call(
        _cmp_a_kernel, grid=(nb, j1 // tj),
        in_specs=[pl.BlockSpec((1, tj, c), lambda n, i: (n, i, 0)),
                  pl.BlockSpec(w_c1.shape, lambda n, i: (0, 0), pipeline_mode=pl.Buffered(1))],
        out_specs=pl.BlockSpec((1, tj, 512), lambda n, i: (n, i, 0)),
        out_shape=jax.ShapeDtypeStruct((nb, j1, 512), F32),
        compiler_params=_cparams(("parallel", "parallel")),
        name="cmp_a",
    )(x, w_c1)


def _gelu_tanh(x):
    return 0.5 * x * (1.0 + jnp.tanh(math.sqrt(2.0 / math.pi) * (x + 0.044715 * (x * x * x))))


def _cmp_b_kernel(fs_ref, pe_ref, w1_ref, w2_ref, o_ref):
    j1 = fs_ref.shape[1]
    pe = _dot(pe_ref[...].astype(BF16), w1_ref[...])
    pe_term = pe[0:1, 0:256] + pe[1:2, 256:512]
    first = fs_ref[0, :, 0:256]
    nxt = pltpu.roll(fs_ref[0, :, 256:512], j1 - 1, 0)
    h = _gelu_tanh(first + nxt + pe_term)
    o_ref[0] = _dot(h.astype(BF16), w2_ref[...]).astype(BF16)


def _cmp_b(fs, pe_rows, w_c1, w_c2):
    nb, j1, _ = fs.shape
    full = lambda a: pl.BlockSpec(a.shape, lambda n: (0,) * a.ndim)
    return pl.pallas_call(
        _cmp_b_kernel, grid=(nb,),
        in_specs=[pl.BlockSpec((1, j1, 512), lambda n: (n, 0, 0)), full(pe_rows), full(w_c1), full(w_c2)],
        out_specs=pl.BlockSpec((1, j1, SLAB), lambda n: (n, 0, 0)),
        out_shape=jax.ShapeDtypeStruct((nb, j1, SLAB), BF16),
        compiler_params=_cparams(("parallel",)),
        name="cmp_b",
    )(fs, pe_rows, w_c1, w_c2)


def _lut_gather(lut_row, idx):
    m, c = idx.shape
    src = jnp.broadcast_to(lut_row, (m, 128))
    return jnp.concatenate([jnp.take_along_axis(src, idx[:, 128 * k:128 * (k + 1)], axis=1)
                            for k in range(c // 128)], axis=1)


def _masked_softmax(s, valid):
    m = jnp.max(s, axis=1, keepdims=True)
    e = jnp.where(valid, jnp.exp(s - m), 0.0)
    return e / jnp.maximum(jnp.sum(e, axis=1, keepdims=True), 1e-30)


def _select_blocks(imp, qpos):
    b = lax.broadcasted_iota(jnp.int32, imp.shape, 1)
    cur = qpos >> 6
    forced = (b == 0) | (b == cur) | (b == cur - 1)
    score = jnp.where(forced, -NEG, jnp.where(b <= cur, imp, NEG))
    sel = jnp.zeros(imp.shape, F32)
    for _ in range(SEL_TOP):
        m = jnp.max(score, axis=1, keepdims=True)
        first = jnp.min(jnp.where(score == m, b, 1 << 20), axis=1, keepdims=True)
        hit = (b == first) & (m > 0.5 * NEG)
        sel = jnp.where(hit, 1.0, sel)
        score = jnp.where(hit, NEG, score)
    return sel


def _nsa_prompt_kernel(q_ref, kv_ref, cmp_ref, g_ref, lut_ref, cover_ref, e_ref, bt_ref, far_ref, o_ref,
                       m_ref, l_ref, acc_ref, oc_ref, os_ref):
    tq = q_ref.shape[1]
    j1 = cmp_ref.shape[1]
    i = pl.program_id(1)
    nh = N_HEADS
    q4 = jnp.concatenate([q_ref[0, :, SLAB * h:SLAB * (h + 1)] for h in range(nh)], axis=0)
    rowq = lax.broadcasted_iota(jnp.int32, (tq, tq), 0)
    col = lax.broadcasted_iota(jnp.int32, (tq, tq), 1)
    rep = lambda x: jnp.concatenate([x] * nh, axis=0)
    qpos = i * tq + lax.broadcasted_iota(jnp.int32, (tq, 1), 0)

    cmp = cmp_ref[0]
    jcol = lax.broadcasted_iota(jnp.int32, (tq, j1), 1)
    rel_c = qpos - (jcol * CMP_STRIDE + (CMP_LEN - 1))
    valid = (rel_c >= 0) & (jcol < j1 - 1)
    idx = jnp.clip(rel_c, 0, 127)
    bias = jnp.concatenate([_lut_gather(lut_ref[h:h + 1, :], idx) for h in range(nh)], axis=0)
    valid4 = rep(valid)
    pc = _masked_softmax(jnp.where(valid4, _dot_nt(q4, cmp) + bias, NEG), valid4)
    oc_ref[...] = _dot(pc.astype(BF16), cmp)
    psum = pc[0:tq] + pc[tq:2 * tq] + pc[2 * tq:3 * tq] + pc[3 * tq:4 * tq]
    imp = _split_dot(psum, cover_ref[...])
    selb = _select_blocks(imp, qpos).astype(BF16)

    def kv_tile(j, part):
        return kv_ref[0, pl.ds(pl.multiple_of(j * tq, tq), tq), SLAB * part:SLAB * (part + 1)]

    kvt = kv_tile(i, 1)
    ok = (_dot(selb, e_ref[i]) > 0.5) & (col <= rowq)
    s = jnp.where(rep(ok), _dot_nt(q4, kvt) + bt_ref[0], NEG)
    _softmax_step(s, kvt, m_ref, l_ref, acc_ref, True)

    def body(k, _):
        j = i - 1 - k
        kvt = kv_tile(j, 1)
        bias = jnp.where(k == 0, bt_ref[1], far_ref[...])
        ok = _dot(selb, e_ref[j]) > 0.5
        s = jnp.where(rep(ok), _dot_nt(q4, kvt) + bias, NEG)
        _softmax_step(s, kvt, m_ref, l_ref, acc_ref, False)
        return 0

    lax.fori_loop(0, i, body, 0)
    os_ref[...] = acc_ref[...] / jnp.maximum(l_ref[...], 1e-30)

    kvt = kv_tile(i, 2)
    s = jnp.where(rep(col <= rowq), _dot_nt(q4, kvt) + bt_ref[0], NEG)
    _softmax_step(s, kvt, m_ref, l_ref, acc_ref, True)

    @pl.when(i >= 1)
    def _():
        kvt = kv_tile(i - 1, 2)
        _softmax_step(_dot_nt(q4, kvt) + bt_ref[1], kvt, m_ref, l_ref, acc_ref, False)

    @pl.when(i >= 2)
    def _():
        kvt = kv_tile(i - 2, 2)
        s = jnp.where(rep(col > rowq), _dot_nt(q4, kvt) + far_ref[...], NEG)
        _softmax_step(s, kvt, m_ref, l_ref, acc_ref, False)

    ow = acc_ref[...] / jnp.maximum(l_ref[...], 1e-30)
    g = g_ref[0]
    for h in range(nh):
        r = slice(h * tq, (h + 1) * tq)
        o = (g[:, h:h + 1] * oc_ref[r, :] + g[:, nh + h:nh + h + 1] * os_ref[r, :]
             + g[:, 2 * nh + h:2 * nh + h + 1] * ow[r])
        o_ref[0, :, SLAB * h:SLAB * (h + 1)] = o.astype(BF16)


def _cover(j1, nb_pad):
    j = np.arange(j1)[:, None]
    b = np.arange(nb_pad)[None, :]
    cstart, cend, bstart = j * CMP_STRIDE, j * CMP_STRIDE + CMP_LEN - 1, b * SEL_BLOCK
    return jnp.asarray(((cstart < bstart + SEL_BLOCK) & (cend >= bstart)).astype(np.float32), dtype=BF16)


def _expand(n_tiles, tk, nb_pad):
    key = np.arange(n_tiles * tk).reshape(n_tiles, 1, tk)
    b = np.arange(nb_pad).reshape(1, nb_pad, 1)
    return jnp.asarray((key // SEL_BLOCK == b).astype(np.float32), dtype=BF16)


def _nsa_prompt(q, kvb, cmp, gates, lut, bt, far):
    nb, t, _ = q.shape
    tq = TILE
    assert WINDOW == 2 * tq and t // SEL_BLOCK <= 128 and t // SEL_BLOCK >= SEL_TOP
    j1 = cmp.shape[1]
    cover = _cover(j1, 128)
    e = _expand(t // tq, tq, 128)
    full = lambda a: pl.BlockSpec(a.shape, lambda n, i: (0,) * a.ndim)
    return pl.pallas_call(
        _nsa_prompt_kernel, grid=(nb, t // tq),
        in_specs=[pl.BlockSpec((1, tq, 512), lambda n, i: (n, i, 0)),
                  pl.BlockSpec((1, t, 384), lambda n, i: (n, 0, 0)),
                  pl.BlockSpec((1, j1, SLAB), lambda n, i: (n, 0, 0)),
                  pl.BlockSpec((1, tq, 128), lambda n, i: (n, i, 0)),
                  full(lut), full(cover), full(e), full(bt), full(far)],
        out_specs=pl.BlockSpec((1, tq, 512), lambda n, i: (n, i, 0)),
        out_shape=jax.ShapeDtypeStruct((nb, t, 512), BF16),
        scratch_shapes=[pltpu.VMEM((N_HEADS * tq, 1), F32), pltpu.VMEM((N_HEADS * tq, 1), F32),
                        pltpu.VMEM((N_HEADS * tq, SLAB), F32), pltpu.VMEM((N_HEADS * tq, SLAB), F32),
                        pltpu.VMEM((N_HEADS * tq, SLAB), F32)],
        compiler_params=_cparams(("parallel", "arbitrary")),
        name="nsa_prompt",
    )(q, kvb, cmp, gates, lut, cover, e, bt, far)


def kernel(x_prompt, x_sample, cache_sb_kv, cache_nsa_kv, state_nsa_win, cache_mla, cache_moba_kv, page_table, c_prompt, c_sample, rel_bias, ln_in_g, ln_in_b, w_ada, b_ada, w_in, nsa_cmp_pe, nsa_cmp_w1, nsa_cmp_w2, mla_q_norm, mla_kv_norm, mla_w_uq, mla_w_uk, mla_w_uv, w_br, w_o, ln1_g, ln1_b, w_ff1, w_ff3, w_ff2, ln2_g, ln2_b):
    w = dict(w_ada=w_ada, b_ada=b_ada, w_in=w_in, nsa_cmp_pe=nsa_cmp_pe, nsa_cmp_w1=nsa_cmp_w1, nsa_cmp_w2=nsa_cmp_w2,
             mla_q_norm=mla_q_norm, mla_kv_norm=mla_kv_norm, mla_w_uq=mla_w_uq, mla_w_uk=mla_w_uk,
             mla_w_uv=mla_w_uv, w_br=w_br, w_o=w_o, ln1_g=ln1_g, ln1_b=ln1_b, w_ff1=w_ff1, w_ff3=w_ff3,
             w_ff2=w_ff2, ln2_g=ln2_g, ln2_b=ln2_b, ln_in_g=ln_in_g, ln_in_b=ln_in_b, rel_bias=rel_bias)
    packs = [_pack_layer(l, w_in, mla_w_uq, mla_w_uk, mla_w_uv, w_br, nsa_cmp_pe, nsa_cmp_w1, nsa_cmp_w2)
             for l in range(DEPTH)]
    tabs = _bias_tables(rel_bias)
    y_p, st_p = _prompt_trunk(x_prompt, c_prompt, w, packs, tabs)
    pools = (cache_sb_kv, cache_nsa_kv, state_nsa_win, cache_mla, cache_moba_kv)
    y_s, st_s = _decode_trunk(x_sample, c_sample, pools, page_table, w, packs, tabs)
    out = [y_p, y_s]
    for a, b in zip(st_p, st_s):
        out += [a, b]
    return tuple(out)


def _decode_trunk(x, c, pools, page_table, w, packs, tabs):
    n_seq, tt, _ = x.shape
    assert tt <= 8 and tt & (tt - 1) == 0
    sb_pool, nsa_pool, win_state, mla_pool, moba_pool = pools
    n_pool = sb_pool.shape[1]
    n_pages = page_table.shape[1]
    past_len = n_pages * PAGE
    assert past_len % MOBA_BLOCK == 0 and past_len >= WINDOW and past_len // SEL_BLOCK + 1 <= 256
    n_chunks = n_pages // G_PAGES
    rows = 8 * N_HEADS
    lut, _ = tabs
    far_nsa = jnp.repeat(w["rel_bias"][NUM_BUCKETS - 1, 0:N_HEADS], 8).reshape(rows, 1)
    far_moba = jnp.repeat(w["rel_bias"][NUM_BUCKETS - 1, N_HEADS:2 * N_HEADS], 8).reshape(rows, 1)
    sb_pool = sb_pool.reshape(DEPTH, n_pool, PAGE, 256)
    nsa_pool4 = nsa_pool.reshape(DEPTH, n_pool, PAGE, 256)
    nsa_chunks = nsa_pool.reshape(DEPTH, n_pool, PAGE // CMP_STRIDE, CMP_STRIDE * 256)
    moba_pool = moba_pool.reshape(DEPTH, n_pool, PAGE, 256)
    win_state = win_state.reshape(DEPTH, n_seq, win_state.shape[2], SLAB)
    tri = _tri(PAGE)
    cover = _cover(past_len // CMP_STRIDE, 256)
    vm = pltpu.VMEM
    r = n_seq * tt
    xf = x.reshape(1, r, D_MODEL)
    per_seq = lambda a: a.reshape(n_seq, tt, a.shape[-1])
    new = ([], [], [], [], [])
    for l in range(DEPTH):
        pk = packs[l]
        mod = _ada(c, w["w_ada"][l], w["b_ada"][l])
        mod = jnp.repeat(mod, tt, axis=0).reshape(1, r, 6 * D_MODEL)
        res = _proj(xf, mod, w["ln_in_g"], w["ln_in_b"], pk, w["mla_q_norm"][l], w["mla_kv_norm"][l],
                    apply_ln=(l == 0), pos0=past_len, period=tt, mod_per_row=True, emit_kbar=False)
        (sbq, sbkv, sbkvb, nsaq, nsakv, nsawin, nsakvb, nsag, mlar, mlarb, mlaq, mobaq, mobakv, mobakvb) = res[:14]
        if l == 0:
            xf = res[14]
        o_sb = _paged_call(
            _sb_dec_kernel, "sb_dec", page_table, sb_pool, l, 256, True, [per_seq(sbq), per_seq(sbkvb)], [tri], 512,
            [vm((rows, 256), F32), vm((PAGE, 256), F32), vm((rows, 256), F32), vm((rows, 1), F32)])
        o_mla = _paged_call(
            _mla_dec_kernel, "mla_dec", page_table, mla_pool, l, KV_LORA + ROPE, False,
            [per_seq(mlaq), per_seq(mlarb)], [pk["w_uv"]], 512,
            [vm((rows, 256), F32), vm((PAGE, 256), F32), vm((rows, 1), F32), vm((rows, 1), F32),
             vm((rows, KV_LORA), F32)])
        o_moba = _paged_call(
            functools.partial(_moba_dec_kernel, past_len=past_len), "moba_dec", page_table, moba_pool, l, 256, False,
            [per_seq(mobaq), per_seq(mobakvb)], [lut, far_moba], 512,
            [vm((rows, 256), F32), vm((PAGE, 256), F32)] + [vm((n_chunks, rows, 128), F32)] * 3
            + [vm((n_chunks * G_PAGES // 2, rows, 256), F32), vm((rows, MOBA_BLOCK), F32)])
        fs = _cmp_a_dec(page_table, nsa_chunks, l, pk["w_c1"])
        cmp = _cmp_b(fs, pk["pe_rows"], pk["w_c1"], pk["w_c2"])
        o_nsa = _paged_call(
            functools.partial(_nsa_dec_kernel, past_len=past_len), "nsa_dec", page_table, nsa_pool4, l, 256, False,
            [per_seq(nsaq), per_seq(nsakvb), cmp, per_seq(nsag), win_state[l]], [lut, cover, far_nsa], 512,
            [vm((rows, SLAB), F32), vm((PAGE, SLAB), F32), vm((PAGE, SLAB), F32), vm((rows, 128), F32),
             vm((n_chunks, 8, 128), F32), vm((rows, PAGE), F32), vm((rows, 1), F32), vm((rows, 1), F32),
             vm((rows, SLAB), F32), vm((rows, SLAB), F32), vm((rows, SLAB), F32)])
        flat = lambda o: o.reshape(1, r, 512)
        xf = _dense_tail(l, xf, mod, True, (flat(o_sb), flat(o_nsa), flat(o_mla), flat(o_moba)), w, pk)
        wl = win_state.shape[2]
        win_new = jnp.concatenate([win_state[l], per_seq(nsawin)], axis=1)[:, tt:]
        rows_out = (sbkv.reshape(n_seq, tt, 2, 2, HD), nsakv.reshape(n_seq, tt, 2, 2, HD),
                    win_new.reshape(n_seq, wl, 2, HD), per_seq(mlar), mobakv.reshape(n_seq, tt, 2, 2, HD))
        for lst, a in zip(new, rows_out):
            lst.append(a)
    return xf.reshape(n_seq, tt, D_MODEL), [jnp.stack(lst, axis=0) for lst in new]


G_PAGES = 16


def _page_specs(l, lanes, n_chunks, reverse):
    def spec(g):
        if reverse:
            return pl.BlockSpec((None, None, PAGE, lanes),
                                lambda n, c, pt: (l, pt[n, (n_chunks - 1 - c) * G_PAGES + g], 0, 0))
        return pl.BlockSpec((None, None, PAGE, lanes), lambda n, c, pt: (l, pt[n, c * G_PAGES + g], 0, 0))
    return [spec(g) for g in range(G_PAGES)]


def _seq_spec(shape):
    nd = len(shape)
    return pl.BlockSpec((1,) + tuple(shape[1:]), lambda n, c, pt: (n,) + (0,) * (nd - 1))


def _const_spec(a):
    return pl.BlockSpec(a.shape, lambda n, c, pt: (0,) * a.ndim)


def _paged_call(kernel, name, page_table, cache, l, lanes, reverse, seq_ins, const_ins, out_lanes, scratch):
    n_seq, n_pages = page_table.shape
    assert n_pages % G_PAGES == 0
    n_chunks = n_pages // G_PAGES
    tt = seq_ins[0].shape[1]
    grid_spec = pltpu.PrefetchScalarGridSpec(
        num_scalar_prefetch=1, grid=(n_seq, n_chunks),
        in_specs=[_seq_spec(a.shape) for a in seq_ins] + [_const_spec(a) for a in const_ins]
        + _page_specs(l, lanes, n_chunks, reverse),
        out_specs=_seq_spec((n_seq, tt, out_lanes)),
        scratch_shapes=scratch)
    return pl.pallas_call(
        functools.partial(kernel, n_seq_ins=len(seq_ins), n_const=len(const_ins)),
        grid_spec=grid_spec, out_shape=jax.ShapeDtypeStruct((n_seq, tt, out_lanes), F32),
        compiler_params=_cparams(("parallel", "arbitrary")), name=name,
    )(page_table, *seq_ins, *const_ins, *([cache] * G_PAGES))


def _fill_rows(dst_ref, src, width, lane_of_head):
    dst_ref[...] = jnp.zeros(dst_ref.shape, F32)
    tt = src.shape[0]
    for h in range(N_HEADS):
        lo = lane_of_head(h)
        dst_ref[8 * h:8 * h + tt, lo:lo + width] = src[:, width * h:width * (h + 1)].astype(F32)


def _fill_new(dst_ref, src):
    dst_ref[...] = jnp.zeros(dst_ref.shape, F32)
    dst_ref[0:src.shape[0], :] = src.astype(F32)


def _row_t(shape):
    return lax.broadcasted_iota(jnp.int32, shape, 0) & 7


def _head_lut(lut_ref, head0, idx):
    return jnp.concatenate([_lut_gather(lut_ref[head0 + h:head0 + h + 1, :], idx[8 * h:8 * h + 8])
                            for h in range(N_HEADS)], axis=0)


def _sb_dec_kernel(pt_ref, *refs, n_seq_ins, n_const):
    q_ref, new_ref, tri_ref = refs[:3]
    pages = refs[3:3 + G_PAGES]
    o_ref, q32_ref, new32_ref, acc_ref, carry_ref = refs[3 + G_PAGES:]
    c = pl.program_id(1)
    tt = q_ref.shape[1]
    tri = tri_ref[...]
    rt = _row_t((8 * N_HEADS, PAGE))
    col = lax.broadcasted_iota(jnp.int32, (8 * N_HEADS, PAGE), 1)

    @pl.when(c == 0)
    def _():
        _fill_rows(q32_ref, q_ref[0], SLAB, lambda h: SLAB * (h // 2))
        _fill_new(new32_ref, new_ref[0])
        kvt = new32_ref[...].astype(BF16)
        z = _dot_nt(q32_ref[...].astype(BF16), kvt)
        contrib, carry = _sb_tile(z, kvt, tri, jnp.zeros((8 * N_HEADS, 1), F32), col < rt)
        acc_ref[...] = contrib
        carry_ref[...] = carry

    q = q32_ref[...].astype(BF16)
    kv = [pages[g][...].astype(BF16) for g in range(G_PAGES)]
    zs = [_dot_nt(q, kv[g]) for g in range(G_PAGES)]
    lrs = [_neg_softplus(z) for z in zs]
    tails = _split_dot(jnp.concatenate(lrs, axis=0), tri)
    carry = carry_ref[...]
    ws = [None] * G_PAGES
    for g in reversed(range(G_PAGES)):
        tail = tails[8 * N_HEADS * g:8 * N_HEADS * (g + 1)]
        ws[g] = jnp.exp(zs[g] + lrs[g] + tail + carry).astype(BF16)
        carry = carry + tail[:, 0:1] + lrs[g][:, 0:1]
    carry_ref[...] = carry
    acc_ref[...] += _dot(jnp.concatenate(ws, axis=1), jnp.concatenate(kv, axis=0))

    @pl.when(c == pl.num_programs(1) - 1)
    def _():
        acc = acc_ref[...]
        for h in range(N_HEADS):
            o_ref[0, :, SLAB * h:SLAB * (h + 1)] = acc[8 * h:8 * h + tt, SLAB * (h // 2):SLAB * (h // 2 + 1)]


def _mla_dec_kernel(pt_ref, *refs, n_seq_ins, n_const):
    q_ref, new_ref, wuv_ref = refs[:3]
    pages = refs[3:3 + G_PAGES]
    o_ref, q32_ref, new32_ref, m_ref, l_ref, acc_ref = refs[3 + G_PAGES:]
    c = pl.program_id(1)
    tt = q_ref.shape[1]
    rt = _row_t((8 * N_HEADS, PAGE))
    col = lax.broadcasted_iota(jnp.int32, (8 * N_HEADS, PAGE), 1)

    @pl.when(c == 0)
    def _():
        _fill_rows(q32_ref, q_ref[0], 256, lambda h: 0)
        _fill_new(new32_ref, new_ref[0])
        rows = new32_ref[...].astype(BF16)
        s = jnp.where((col <= rt) & (col < tt), _dot_nt(q32_ref[...].astype(BF16), rows), NEG)
        _softmax_step(s, rows[:, 0:KV_LORA], m_ref, l_ref, acc_ref, True)

    q = q32_ref[...].astype(BF16)
    ss, vs = [], []
    for g in range(G_PAGES):
        pg = pages[g][...]
        lat = pg[:, 0:KV_LORA].astype(BF16)
        ss.append(_dot_nt(q[:, 0:KV_LORA], lat)
                  + _dot_nt(q[:, KV_LORA:KV_LORA + ROPE], pg[:, KV_LORA:KV_LORA + ROPE].astype(BF16)))
        vs.append(lat)
    _softmax_step(jnp.concatenate(ss, axis=1), jnp.concatenate(vs, axis=0), m_ref, l_ref, acc_ref, False)

    @pl.when(c == pl.num_programs(1) - 1)
    def _():
        olat = (acc_ref[...] / jnp.maximum(l_ref[...], 1e-30)).astype(BF16)
        for h in range(N_HEADS):
            o_ref[0, :, SLAB * h:SLAB * (h + 1)] = _dot(olat[8 * h:8 * h + 8], wuv_ref[h])[0:tt]


def _moba_dec_kernel(pt_ref, *refs, n_seq_ins, n_const, past_len):
    q_ref, new_ref, lut_ref, far_ref = refs[:4]
    pages = refs[4:4 + G_PAGES]
    (o_ref, q32_ref, new32_ref, gs_ref, mb_ref, lb_ref, accb_ref, blast_ref) = refs[4 + G_PAGES:]
    c = pl.program_id(1)
    n_chunks = gs_ref.shape[0]
    bpc = G_PAGES // 2
    tt = q_ref.shape[1]
    rows = 8 * N_HEADS
    rt = _row_t((rows, PAGE))
    col = lax.broadcasted_iota(jnp.int32, (rows, PAGE), 1)
    far = far_ref[...]

    @pl.when(c == 0)
    def _():
        _fill_rows(q32_ref, q_ref[0], SLAB, lambda h: SLAB * (h // 2))
        _fill_new(new32_ref, new_ref[0])
        gs_ref[...] = jnp.zeros(gs_ref.shape, F32)
        mb_ref[...] = jnp.zeros(mb_ref.shape, F32)
        lb_ref[...] = jnp.zeros(lb_ref.shape, F32)
        t2 = _row_t((rows, MOBA_BLOCK))
        s2 = lax.broadcasted_iota(jnp.int32, (rows, MOBA_BLOCK), 1)
        blast_ref[...] = _head_lut(lut_ref, N_HEADS, jnp.clip(MOBA_BLOCK + t2 - s2, 0, 127))

    qf = q32_ref[...]
    q = qf.astype(BF16)
    for k in range(bpc):
        p0, p1 = pages[2 * k][...], pages[2 * k + 1][...]
        kbar = (jnp.sum(p0, axis=0, keepdims=True) + jnp.sum(p1, axis=0, keepdims=True)) * (1.0 / MOBA_BLOCK)
        kv = jnp.concatenate([p0, p1], axis=0).astype(BF16)
        is_last = (c == n_chunks - 1) & (k == bpc - 1)
        s = _dot_nt(q, kv) + jnp.where(is_last, blast_ref[...], far)
        m = jnp.max(s, axis=1, keepdims=True)
        p = jnp.exp(s - m)
        gs_ref[c, :, k:k + 1] = jnp.sum(qf * kbar, axis=1, keepdims=True)
        mb_ref[c, :, k:k + 1] = m
        lb_ref[c, :, k:k + 1] = jnp.sum(p, axis=1, keepdims=True)
        accb_ref[c * bpc + k] = _dot(p.astype(BF16), kv)

    @pl.when(c == n_chunks - 1)
    def _():
        lane = lax.broadcasted_iota(jnp.int32, (rows, 128), 1)

        def gather(ref):
            out = jnp.zeros((rows, 128), F32)
            for cc in range(n_chunks):
                part = jnp.where(lane < bpc, ref[cc], 0.0)
                out = out + (part if cc == 0 else pltpu.roll(part, cc * bpc, 1))
            return out

        nblk = n_chunks * bpc
        gs = jnp.where(lane < nblk, gather(gs_ref), NEG)
        ids = _topk_ids(gs, MOBA_TOP)
        sel = (lane == ids[0]) | (lane == ids[1]) | (lane == ids[2])
        m_all = gather(mb_ref)
        l_all = gather(lb_ref)
        kvn = new32_ref[...].astype(BF16)
        bias_n = _head_lut(lut_ref, N_HEADS, jnp.clip(rt - col, 0, 127))
        s_n = jnp.where((col <= rt) & (col < tt), _dot_nt(q, kvn) + bias_n, NEG)
        m_n = jnp.max(s_n, axis=1, keepdims=True)
        m_fin = jnp.maximum(m_n, jnp.max(jnp.where(sel, m_all, NEG), axis=1, keepdims=True))
        p_n = jnp.exp(s_n - m_fin)
        wgt = jnp.where(sel, jnp.exp(m_all - m_fin), 0.0)
        l_fin = jnp.sum(p_n, axis=1, keepdims=True) + jnp.sum(wgt * l_all, axis=1, keepdims=True)
        acc = _dot(p_n.astype(BF16), kvn)
        for b in range(nblk):
            acc = acc + wgt[:, b:b + 1] * accb_ref[b]
        out = acc / jnp.maximum(l_fin, 1e-30)
        for h in range(N_HEADS):
            o_ref[0, :, SLAB * h:SLAB * (h + 1)] = out[8 * h:8 * h + tt, SLAB * (h // 2):SLAB * (h // 2 + 1)]


def _cmp_a_dec_kernel(pt_ref, *refs, n_seq_ins, n_const):
    w_ref = refs[1]
    pages = refs[2:2 + G_PAGES]
    o_ref = refs[2 + G_PAGES]
    x = jnp.concatenate([pages[g][...] for g in range(G_PAGES)], axis=0).astype(BF16)
    o_ref[0] = _dot(x, w_ref[...])


def _cmp_a_dec(page_table, cache_view, l, w_c1):
    n_seq, n_pages = page_table.shape
    n_chunks = n_pages // G_PAGES
    rows = PAGE // CMP_STRIDE
    dummy = jnp.zeros((n_seq, 8, 128), F32)
    grid_spec = pltpu.PrefetchScalarGridSpec(
        num_scalar_prefetch=1, grid=(n_seq, n_chunks),
        in_specs=[_seq_spec(dummy.shape), _const_spec(w_c1)]
        + [pl.BlockSpec((None, None, rows, CMP_STRIDE * 256), lambda n, c, pt, g=g: (l, pt[n, c * G_PAGES + g], 0, 0))
           for g in range(G_PAGES)],
        out_specs=pl.BlockSpec((1, rows * G_PAGES, 512), lambda n, c, pt: (n, c, 0)))
    return pl.pallas_call(
        functools.partial(_cmp_a_dec_kernel, n_seq_ins=1, n_const=1), grid_spec=grid_spec,
        out_shape=jax.ShapeDtypeStruct((n_seq, rows * n_pages, 512), F32),
        compiler_params=_cparams(("parallel", "arbitrary")), name="cmp_a_dec",
    )(page_table, dummy, w_c1, *([cache_view] * G_PAGES))


def _nsa_dec_kernel(pt_ref, *refs, n_seq_ins, n_const, past_len):
    q_ref, new_ref, cmp_ref, g_ref, win_ref, lut_ref, cover_ref, far_ref = refs[:8]
    pages = refs[8:8 + G_PAGES]
    (o_ref, q32_ref, newsel_ref, newwin_ref, g32_ref, sel_ref, blast_ref, m_ref, l_ref, acc_ref,
     oc_ref, ow_ref) = refs[8 + G_PAGES:]
    c = pl.program_id(1)
    n_chunks = sel_ref.shape[0]
    n_pages = n_chunks * G_PAGES
    tt = q_ref.shape[1]
    nh = N_HEADS
    rows = 8 * nh
    rt = _row_t((rows, PAGE))
    col = lax.broadcasted_iota(jnp.int32, (rows, PAGE), 1)
    far = far_ref[...]
    rep = lambda x: jnp.concatenate([x] * nh, axis=0)

    @pl.when(c == 0)
    def _():
        _fill_rows(q32_ref, q_ref[0], SLAB, lambda h: 0)
        new = new_ref[0]
        _fill_new(newsel_ref, new[:, SLAB:2 * SLAB])
        _fill_new(newwin_ref, new[:, 2 * SLAB:3 * SLAB])
        g32_ref[...] = jnp.zeros(g32_ref.shape, F32)
        for h in range(nh):
            g32_ref[8 * h:8 * h + tt, :] = g_ref[0]
        q = q32_ref[...].astype(BF16)
        cmp = cmp_ref[0]
        j1 = cmp.shape[0]
        jcol = lax.broadcasted_iota(jnp.int32, (rows, j1), 1)
        rel_c = past_len + _row_t((rows, j1)) - (jcol * CMP_STRIDE + (CMP_LEN - 1))
        valid = jcol < j1 - 1
        bias = _head_lut(lut_ref, 0, jnp.clip(rel_c, 0, 127))
        pc = _masked_softmax(jnp.where(valid, _dot_nt(q, cmp) + bias, NEG), valid)
        oc_ref[...] = _dot(pc.astype(BF16), cmp)
        psum = pc[0:8] + pc[8:16] + pc[16:24] + pc[24:32]
        imp = _split_dot(psum, cover_ref[...])
        qpos = past_len + lax.broadcasted_iota(jnp.int32, (8, 1), 0)
        sel = _select_blocks(imp, qpos)
        per = 2 * G_PAGES
        sel_ref[...] = jnp.zeros(sel_ref.shape, F32)
        for cc in range(n_chunks):
            sel_ref[cc, :, 0:per] = sel[:, cc * per:(cc + 1) * per]
        blast_ref[...] = _head_lut(lut_ref, 0, jnp.clip(PAGE + rt - col, 0, 127))
        wst = win_ref[0].astype(BF16)
        wl = wst.shape[0]
        tw = _row_t((rows, wl))
        iw = lax.broadcasted_iota(jnp.int32, (rows, wl), 1)
        s = _dot_nt(q, wst) + _head_lut(lut_ref, 0, jnp.clip(wl + tw - iw, 0, 127))
        _softmax_step(jnp.where(iw > tw, s, NEG), wst, m_ref, l_ref, acc_ref, True)
        bias_n = _head_lut(lut_ref, 0, jnp.clip(rt - col, 0, 127))
        ok_n = (col <= rt) & (col < tt)
        wn = newwin_ref[...].astype(BF16)
        _softmax_step(jnp.where(ok_n, _dot_nt(q, wn) + bias_n, NEG), wn, m_ref, l_ref, acc_ref, False)
        ow_ref[...] = acc_ref[...] / jnp.maximum(l_ref[...], 1e-30)
        sn = newsel_ref[...].astype(BF16)
        _softmax_step(jnp.where(ok_n, _dot_nt(q, sn) + bias_n, NEG), sn, m_ref, l_ref, acc_ref, True)

    q = q32_ref[...].astype(BF16)
    selc = sel_ref[c]
    ss, vs = [], []
    for g in range(G_PAGES):
        kv = pages[g][:, SLAB:2 * SLAB].astype(BF16)
        chosen = jnp.where(col[0:8] < SEL_BLOCK, selc[:, 2 * g:2 * g + 1], selc[:, 2 * g + 1:2 * g + 2]) > 0.5
        is_last = (c == n_chunks - 1) & (g == G_PAGES - 1)
        s = _dot_nt(q, kv) + jnp.where(is_last, blast_ref[...], far)
        ss.append(jnp.where(rep(chosen), s, NEG))
        vs.append(kv)
    _softmax_step(jnp.concatenate(ss, axis=1), jnp.concatenate(vs, axis=0), m_ref, l_ref, acc_ref, False)

    @pl.when(c == n_chunks - 1)
    def _():
        osel = acc_ref[...] / jnp.maximum(l_ref[...], 1e-30)
        lane = lax.broadcasted_iota(jnp.int32, (rows, 128), 1)
        head = lax.broadcasted_iota(jnp.int32, (rows, 128), 0) >> 3
        g32 = g32_ref[...]
        gcol = lambda k: jnp.sum(jnp.where(lane == k * nh + head, g32, 0.0), axis=1, keepdims=True)
        out = gcol(0) * oc_ref[...] + gcol(1) * osel + gcol(2) * ow_ref[...]
        for h in range(nh):
            o_ref[0, :, SLAB * h:SLAB * (h + 1)] = out[8 * h:8 * h + tt]


def _dense_tail(l, x, mod, mod_per_row, outs, w, pk):
    row = lambda a: a[l].reshape(1, -1)
    x = _row_call(_merge_kernel, "merge", x, mod, mod_per_row, list(outs),
                  [pk["w_gate"], pk["w_brp"], w["w_o"][l].astype(BF16), row(w["ln1_g"]), row(w["ln1_b"])])
    return _row_call(_ffn_kernel, "ffn", x, mod, mod_per_row, [],
                     [w["w_ff1"][l].astype(BF16), w["w_ff3"][l].astype(BF16), w["w_ff2"][l].astype(BF16),
                      row(w["ln2_g"]), row(w["ln2_b"])])


def _prompt_trunk(x, c, w, packs, tabs):
    nb, t, _ = x.shape
    lut, tiles = tabs
    bt_nsa, far_nsa = _group_tiles(tiles[0:4], w["rel_bias"][NUM_BUCKETS - 1, 0:4], 1)
    bt_moba, far_moba = _group_tiles(tiles[4:8], w["rel_bias"][NUM_BUCKETS - 1, 4:8], 2)
    new = ([], [], [], [], [])
    for l in range(DEPTH):
        pk = packs[l]
        mod = _ada(c, w["w_ada"][l], w["b_ada"][l]).reshape(nb, 1, 6 * D_MODEL)
        res = _proj(x, mod, w["ln_in_g"], w["ln_in_b"], pk, w["mla_q_norm"][l], w["mla_kv_norm"][l],
                    apply_ln=(l == 0), pos0=0, period=t, mod_per_row=False, emit_kbar=True)
        (sbq, sbkv, sbkvb, nsaq, nsakv, nsawin, nsakvb, nsag, mlar, mlarb, mlaq, mobaq, mobakv, mobakvb) = res[:14]
        if l == 0:
            x = res[14]
        kbar = res[-1].reshape(nb, t // TILE, 256)
        kbar = jnp.pad(kbar, ((0, 0), (0, 128 - kbar.shape[1]), (0, 0)))
        o_sb = _sb_prompt(sbq, sbkvb)
        fs = _cmp_a(nsakv.reshape(nb, t // CMP_STRIDE, CMP_STRIDE * 256), pk["w_c1"])
        cmp = _cmp_b(fs, pk["pe_rows"], pk["w_c1"], pk["w_c2"])
        o_nsa = _nsa_prompt(nsaq, nsakvb, cmp, nsag, lut[0:4], bt_nsa[0], far_nsa[0])
        o_mla = _mla_prompt(mlaq, mlarb, pk["w_uv"])
        o_moba = _moba_prompt(mobaq, mobakvb, kbar, bt_moba, far_moba)
        x = _dense_tail(l, x, mod, False, (o_sb, o_nsa, o_mla, o_moba), w, pk)
        wl = min(WINDOW, t)
        rows = (sbkv.reshape(nb, t, 2, 2, HD), nsakv.reshape(nb, t, 2, 2, HD),
                nsawin[:, t - wl:].reshape(nb, wl, 2, HD), mlar, mobakv.reshape(nb, t, 2, 2, HD))
        for lst, r in zip(new, rows):
            lst.append(r)
    return x, [jnp.stack(lst, axis=0) for lst in new]
```

```python
import functools
import math

import numpy as np
import jax
import jax.numpy as jnp
from jax import lax
from jax.experimental import pallas as pl
from jax.experimental.pallas import tpu as pltpu

F32 = jnp.float32
BF16 = jnp.bfloat16

D_MODEL = 1024
PAGE = 128
HD = 64
SLAB = 2 * HD
N_HEADS = 4
CMP_STRIDE = 16
CMP_LEN = 32
CMP_HIDDEN = 128
SEL_BLOCK = 64
SEL_TOP = 16
WINDOW = 512
Q_LORA = 256
KV_LORA = 128
NOPE = 64
ROPE = 32
ROPE_THETA = 10000.0
MOBA_BLOCK = 256
MOBA_TOP = 3
NUM_BUCKETS = 32
MAX_DISTANCE = 128
DEPTH = 2
D_FF = 2816
ALPHA = (2 * DEPTH) ** 0.25
N_IN_MIX = 2092
TILE = 256
NEG = -1e30
VMEM_LIMIT = 56 * 1024 * 1024

C_SBQ, C_SBKV, C_NSAQ, C_NSAKV, C_NSAG, C_QA, C_KVA, C_KR, C_KRS, C_MOBAQ, C_MOBAKV, C_END = (
    0, 512, 768, 1280, 1664, 1792, 2048, 2176, 2304, 2432, 2944, 3200)


def _cparams(sem, vmem=VMEM_LIMIT):
    return pltpu.CompilerParams(dimension_semantics=sem, vmem_limit_bytes=vmem)


def _dot(a, b):
    return jnp.dot(a, b, preferred_element_type=F32)


def _dot_nt(a, b):
    return lax.dot_general(a, b, (((1,), (1,)), ((), ())), preferred_element_type=F32)


def _split_dot(x, w):
    hi = x.astype(BF16)
    lo = (x - hi.astype(F32)).astype(BF16)
    return _dot(hi, w) + _dot(lo, w)


def _layer_norm(x, g, b):
    mu = jnp.mean(x, axis=-1, keepdims=True)
    xc = x - mu
    var = jnp.mean(xc * xc, axis=-1, keepdims=True)
    return xc * lax.rsqrt(var + 1e-5) * g + b


def _sigmoid(x):
    return 1.0 / (1.0 + jnp.exp(-x))


def _pad_heads(w):
    k = w.shape[0]
    return jnp.pad(w.reshape(k, N_HEADS, HD), ((0, 0), (0, 0), (0, HD))).reshape(k, N_HEADS * SLAB)


def _pad_cols(w, n):
    return jnp.pad(w, ((0, 0), (0, n - w.shape[1])))


def _pack_layer(l, w_in, mla_w_uq, mla_w_uk, mla_w_uv, w_br, nsa_cmp_pe, nsa_cmp_w1, nsa_cmp_w2):
    wi = w_in[l]
    o = np.cumsum([0, 256, 256, 256, 384, 12, 256, 128, 32, 256, 256]).tolist()
    sb_q, sb_kv, nsa_q, nsa_kv, nsa_g, qa, kva, kr, moba_q, moba_kv = [wi[:, o[i]:o[i + 1]] for i in range(10)]
    kr_sw = jnp.concatenate([kr[:, ROPE // 2:], kr[:, :ROPE // 2]], axis=1)
    w_proj = jnp.concatenate([
        _pad_heads(sb_q), sb_kv, _pad_heads(nsa_q), nsa_kv, _pad_cols(nsa_g, 128), qa, kva,
        _pad_cols(kr, 128), _pad_cols(kr_sw, 128), _pad_heads(moba_q), moba_kv], axis=1).astype(BF16)
    w_gate = wi[:, N_IN_MIX:].astype(BF16)
    uq = mla_w_uq[l]
    uq_nope = uq[:, :, :NOPE].reshape(Q_LORA, N_HEADS * NOPE)
    uq_rope = uq[:, :, NOPE:]
    uq_rope_sw = jnp.concatenate([uq_rope[..., ROPE // 2:], uq_rope[..., :ROPE // 2]], axis=-1)
    place = lambda r: jnp.pad(r, ((0, 0), (0, 0), (0, 128 - ROPE))).reshape(Q_LORA, N_HEADS * 128)
    w_q3 = jnp.concatenate([uq_nope, place(uq_rope), place(uq_rope_sw)], axis=1).astype(BF16)
    uk = mla_w_uk[l]
    w_uk = jnp.zeros((N_HEADS * NOPE, N_HEADS * KV_LORA), F32)
    for h in range(N_HEADS):
        w_uk = w_uk.at[h * NOPE:(h + 1) * NOPE, h * KV_LORA:(h + 1) * KV_LORA].set(uk[:, h, :].T)
    w_uk = w_uk.astype(BF16)
    uv = mla_w_uv[l]
    w_uv = jnp.pad(jnp.transpose(uv, (1, 0, 2)), ((0, 0), (0, 0), (HD, 0))).astype(BF16)
    wb = w_br[l].reshape(4, N_HEADS, HD, D_MODEL)
    w_brp = jnp.pad(wb, ((0, 0), (0, 0), (HD, 0), (0, 0))).reshape(4, N_HEADS * SLAB, D_MODEL).astype(BF16)
    w1 = nsa_cmp_w1[l].reshape(2, 2, CMP_STRIDE, HD, CMP_HIDDEN)
    halves = []
    for half in range(2):
        wh = jnp.zeros((CMP_STRIDE, 4, HD, 2, CMP_HIDDEN), F32)
        for c in range(2):
            wh = wh.at[:, c, :, c, :].set(w1[c, half])
        halves.append(wh.reshape(CMP_STRIDE * 4 * HD, 2 * CMP_HIDDEN))
    w_c1 = jnp.concatenate(halves, axis=1).astype(BF16)
    w_c1d = jnp.concatenate(halves, axis=1).reshape(CMP_STRIDE, 4, HD, 4 * CMP_HIDDEN)[:, 0:2]
    w_c1d = w_c1d.reshape(CMP_STRIDE * 2 * HD, 4 * CMP_HIDDEN).astype(BF16)
    pe = nsa_cmp_pe[l].reshape(2, 2, CMP_STRIDE, HD)
    pe_rows = []
    for half in range(2):
        p = jnp.zeros((CMP_STRIDE, 4, HD), F32)
        for c in range(2):
            p = p.at[:, c, :].set(pe[c, half])
        pe_rows.append(p.reshape(1, CMP_STRIDE * 4 * HD))
    pe_rows = jnp.concatenate(pe_rows + [jnp.zeros((6, CMP_STRIDE * 4 * HD), F32)], axis=0)
    w2 = nsa_cmp_w2[l]
    w_c2 = jnp.zeros((2 * CMP_HIDDEN, 2 * HD), F32)
    for c in range(2):
        w_c2 = w_c2.at[c * CMP_HIDDEN:(c + 1) * CMP_HIDDEN, c * HD:(c + 1) * HD].set(w2[c])
    w_c2 = w_c2.astype(BF16)
    return dict(w_proj=w_proj, w_gate=w_gate, w_q3=w_q3, w_uk=w_uk, w_uv=w_uv, w_brp=w_brp,
                w_c1=w_c1, w_c1d=w_c1d, pe_rows=pe_rows, w_c2=w_c2)


def _rope_freq():
    half = ROPE // 2
    inv = ROPE_THETA ** (-np.arange(half, dtype=np.float32) / half)
    f = np.zeros((1, 128), np.float32)
    f[0, :half] = -inv
    f[0, half:ROPE] = inv
    return jnp.asarray(f)


def _ada_kernel(c_ref, w_ref, b_ref, o_ref):
    c = c_ref[...]
    act = (c * _sigmoid(c)).astype(BF16)
    o_ref[...] = _dot(act, w_ref[...].astype(BF16)) + b_ref[...]


def _ada(c, w, b):
    n = c.shape[0]
    npad = -(-n // 8) * 8
    cp = jnp.pad(c, ((0, npad - n), (0, 0)))
    cols = 6 * D_MODEL
    tn = 1536
    out = pl.pallas_call(
        _ada_kernel,
        grid=(cols // tn,),
        in_specs=[pl.BlockSpec((npad, D_MODEL), lambda j: (0, 0)),
                  pl.BlockSpec((D_MODEL, tn), lambda j: (0, j)),
                  pl.BlockSpec((1, tn), lambda j: (0, j))],
        out_specs=pl.BlockSpec((npad, tn), lambda j: (0, j)),
        out_shape=jax.ShapeDtypeStruct((npad, cols), F32),
        compiler_params=_cparams(("arbitrary",)),
        name="ada",
    )(cp, w, b.reshape(1, cols))
    return out[:n]


def _proj_kernel(x_ref, mod_ref, lng_ref, lnb_ref, w_ref, qn_ref, kvn_ref, wq3_ref, wuk_ref, freq_ref, *outs,
                 apply_ln, pos0, period, emit_kbar):
    outs = list(outs)
    (sbq_o, sbkv_o, sbkvb_o, nsaq_o, nsakv_o, nsawin_o, nsakvb_o, nsag_o, mlar_o, mlarb_o, mlaq_o,
     mobaq_o, mobakv_o, mobakvb_o) = outs[:14]
    rest = outs[14:]
    tm = x_ref.shape[1]
    x = x_ref[0]
    if apply_ln:
        x = _layer_norm(x, lng_ref[...], lnb_ref[...])
        rest.pop(0)[0] = x
    mod = mod_ref[0]
    u = (x * (1.0 + mod[:, D_MODEL:2 * D_MODEL]) + mod[:, 0:D_MODEL]).astype(BF16)

    def grp(a, b):
        return _dot(u, w_ref[:, a:b])

    qscale = HD ** -0.5
    sbq_o[0] = (grp(C_SBQ, C_SBKV) * qscale).astype(BF16)
    kv = grp(C_SBKV, C_NSAQ)
    sbkv_o[0] = kv
    sbkvb_o[0] = kv.astype(BF16)
    nsaq_o[0] = (grp(C_NSAQ, C_NSAKV) * qscale).astype(BF16)
    kv = grp(C_NSAKV, C_NSAG)
    nsakv_o[0] = kv[:, 0:256]
    nsawin_o[0] = kv[:, 256:384]
    nsakvb_o[0] = kv.astype(BF16)
    nsag_o[0] = _sigmoid(grp(C_NSAG, C_QA))
    mobaq_o[0] = (grp(C_MOBAQ, C_MOBAKV) * qscale).astype(BF16)
    kv = grp(C_MOBAKV, C_END)
    mobakv_o[0] = kv
    mobakvb_o[0] = kv.astype(BF16)
    if emit_kbar:
        rest.pop(0)[0, 0] = jnp.mean(kv, axis=0, keepdims=True)

    row = lax.broadcasted_iota(jnp.int32, (tm, 1), 0) + pl.program_id(1) * tm
    pos = (pos0 + (row & (period - 1))).astype(F32)
    ang = pos * freq_ref[...]
    cs, sn = jnp.cos(ang), jnp.sin(ang)

    kva = grp(C_KVA, C_KR)
    ckv = kva * lax.rsqrt(jnp.mean(kva * kva, axis=-1, keepdims=True) + 1e-6) * kvn_ref[...]
    kr = grp(C_KR, C_KRS) * cs + grp(C_KRS, C_MOBAQ) * sn
    mlar_o[0, :, 0:KV_LORA] = ckv
    mlar_o[0, :, KV_LORA:KV_LORA + ROPE] = kr[:, 0:ROPE]
    last = lax.broadcasted_iota(jnp.int32, kr.shape, 1) == 127
    mlarb_o[0] = jnp.concatenate([ckv, jnp.where(last, 1.0, kr)], axis=1).astype(BF16)

    qa = grp(C_QA, C_KVA)
    qan = (qa * lax.rsqrt(jnp.mean(qa * qa, axis=-1, keepdims=True) + 1e-6) * qn_ref[...]).astype(BF16)
    q3 = _dot(qan, wq3_ref[...])
    qlat = _dot(q3[:, 0:256].astype(BF16), wuk_ref[...])
    mscale = (NOPE + ROPE) ** -0.5
    for h in range(N_HEADS):
        a = q3[:, 256 + 128 * h:384 + 128 * h]
        b = q3[:, 768 + 128 * h:896 + 128 * h]
        mlaq_o[0, :, 256 * h:256 * h + 128] = (qlat[:, 128 * h:128 * h + 128] * mscale).astype(BF16)
        mlaq_o[0, :, 256 * h + 128:256 * h + 256] = ((a * cs + b * sn) * mscale).astype(BF16)


def _proj(x, mod, lng, lnb, pk, qn, kvn, *, apply_ln, pos0, period, mod_per_row, emit_kbar):
    nb, r, _ = x.shape
    tm = min(TILE, r)
    grid = (nb, r // tm)
    row = lambda c, dt: (pl.BlockSpec((1, tm, c), lambda n, i: (n, i, 0)), jax.ShapeDtypeStruct((nb, r, c), dt))
    outs = [row(512, BF16), row(256, F32), row(256, BF16), row(512, BF16), row(256, F32), row(128, F32),
            row(384, BF16), row(128, F32), row(KV_LORA + ROPE, F32), row(256, BF16), row(1024, BF16),
            row(512, BF16), row(256, F32), row(256, BF16)]
    if apply_ln:
        outs.append(row(D_MODEL, F32))
    if emit_kbar:
        outs.append((pl.BlockSpec((1, 1, 1, 256), lambda n, i: (n, i, 0, 0)),
                     jax.ShapeDtypeStruct((nb, r // tm, 1, 256), F32)))
    full = lambda a: pl.BlockSpec(a.shape, lambda n, i: (0,) * a.ndim)
    if mod_per_row:
        mod_spec = pl.BlockSpec((1, tm, 6 * D_MODEL), lambda n, i: (n, i, 0))
    else:
        mod_spec = pl.BlockSpec((1, 1, 6 * D_MODEL), lambda n, i: (n, 0, 0))
    freq = _rope_freq()
    args = [x, mod, lng.reshape(1, -1), lnb.reshape(1, -1), pk["w_proj"], qn.reshape(1, -1), kvn.reshape(1, -1),
            pk["w_q3"], pk["w_uk"], freq]
    in_specs = [pl.BlockSpec((1, tm, D_MODEL), lambda n, i: (n, i, 0)), mod_spec] + [full(a) for a in args[2:]]
    res = pl.pallas_call(
        functools.partial(_proj_kernel, apply_ln=apply_ln, pos0=pos0, period=period, emit_kbar=emit_kbar),
        grid=grid, in_specs=in_specs,
        out_specs=[o[0] for o in outs], out_shape=[o[1] for o in outs],
        compiler_params=_cparams(("parallel", "parallel")),
        name="proj",
    )(*args)
    return res


def _merge_kernel(x_ref, mod_ref, o0, o1, o2, o3, wg_ref, wbr_ref, wo_ref, g_ref, b_ref, out_ref):
    x = x_ref[0]
    mod = mod_ref[0]
    u = (x * (1.0 + mod[:, D_MODEL:2 * D_MODEL]) + mod[:, 0:D_MODEL]).astype(BF16)
    merged = None
    for bi, o in enumerate((o0, o1, o2, o3)):
        gate = _sigmoid(_dot(u, wg_ref[:, bi * D_MODEL:(bi + 1) * D_MODEL]))
        y = gate * _dot(o[0].astype(BF16), wbr_ref[bi])
        merged = y if merged is None else merged + y
    mix = _dot(merged.astype(BF16), wo_ref[...])
    out_ref[0] = _layer_norm(ALPHA * x + mod[:, 2 * D_MODEL:3 * D_MODEL] * mix, g_ref[...], b_ref[...])


def _ffn_kernel(x_ref, mod_ref, w1_ref, w3_ref, w2_ref, g_ref, b_ref, out_ref):
    x = x_ref[0]
    mod = mod_ref[0]
    u = (x * (1.0 + mod[:, 4 * D_MODEL:5 * D_MODEL]) + mod[:, 3 * D_MODEL:4 * D_MODEL]).astype(BF16)
    a = _dot(u, w1_ref[...])
    h = (a * _sigmoid(a) * _dot(u, w3_ref[...])).astype(BF16)
    f = _dot(h, w2_ref[...])
    out_ref[0] = _layer_norm(ALPHA * x + mod[:, 5 * D_MODEL:6 * D_MODEL] * f, g_ref[...], b_ref[...])


def _row_call(kernel, name, x, mod, mod_per_row, row_ins, consts):
    nb, r, _ = x.shape
    tm = min(TILE, r)
    if mod_per_row:
        mod_spec = pl.BlockSpec((1, tm, 6 * D_MODEL), lambda n, i: (n, i, 0))
    else:
        mod_spec = pl.BlockSpec((1, 1, 6 * D_MODEL), lambda n, i: (n, 0, 0))
    rspec = lambda a: pl.BlockSpec((1, tm, a.shape[-1]), lambda n, i: (n, i, 0))
    full = lambda a: pl.BlockSpec(a.shape, lambda n, i: (0,) * a.ndim, pipeline_mode=pl.Buffered(1))
    return pl.pallas_call(
        kernel, grid=(nb, r // tm),
        in_specs=[rspec(x), mod_spec] + [rspec(a) for a in row_ins] + [full(a) for a in consts],
        out_specs=rspec(x), out_shape=jax.ShapeDtypeStruct(x.shape, F32),
        compiler_params=_cparams(("parallel", "parallel")),
        name=name,
    )(x, mod, *row_ins, *consts)


def _neg_softplus(z):
    return -(jnp.maximum(z, 0.0) + jnp.log(1.0 + jnp.exp(-jnp.abs(z))))


def _sb_tile(z, kvt, tri, carry, mask):
    lr = _neg_softplus(z)
    if mask is not None:
        lr = jnp.where(mask, lr, 0.0)
    tail = _split_dot(lr, tri)
    w = jnp.exp(z + lr + tail + carry)
    if mask is not None:
        w = jnp.where(mask, w, 0.0)
    contrib = _dot(w.astype(BF16), kvt)
    return contrib, carry + tail[:, 0:1] + lr[:, 0:1]


def _rep2(x):
    return jnp.concatenate([x, x], axis=1)


def _sb_tile_rep(z, kvt, trij, carry, mask):
    tk = z.shape[1]
    lr = _neg_softplus(z)
    if mask is not None:
        lr = jnp.where(mask, lr, 0.0)
    tl = _split_dot(lr, trij)
    w = jnp.exp(z + lr + tl[:, 0:tk] + _rep2(carry))
    if mask is not None:
        w = jnp.where(mask, w, 0.0)
    return _dot(w.astype(BF16), kvt), carry + tl[:, tk:tk + 128]


def _sb_prompt_kernel(q_ref, kv_ref, tri_ref, o_ref, acc_ref, carry_ref):
    tq = q_ref.shape[1]
    i = pl.program_id(1)
    rowq = lax.broadcasted_iota(jnp.int32, (2 * tq, tq), 0) & (tq - 1)
    col = lax.broadcasted_iota(jnp.int32, (2 * tq, tq), 1)
    for kvh in range(2):
        q2 = jnp.concatenate([q_ref[0, :, SLAB * (2 * kvh):SLAB * (2 * kvh + 1)],
                              q_ref[0, :, SLAB * (2 * kvh + 1):SLAB * (2 * kvh + 2)]], axis=0)

        def kv_tile(j):
            return kv_ref[0, pl.ds(pl.multiple_of(j * tq, tq), tq), SLAB * kvh:SLAB * (kvh + 1)]

        kvt = kv_tile(i)
        contrib, carry = _sb_tile_rep(_dot_nt(q2, kvt), kvt, tri_ref[...], jnp.zeros((2 * tq, 128), F32),
                                      col < rowq)
        acc_ref[...] = contrib
        carry_ref[...] = carry

        def body(k, _):
            kvt = kv_tile(i - 1 - k)
            contrib, carry = _sb_tile_rep(_dot_nt(q2, kvt), kvt, tri_ref[...], carry_ref[...], None)
            acc_ref[...] += contrib
            carry_ref[...] = carry
            return 0

        lax.fori_loop(0, i, body, 0)
        acc = acc_ref[...].astype(BF16)
        o_ref[0, :, SLAB * (2 * kvh):SLAB * (2 * kvh + 1)] = acc[0:tq]
        o_ref[0, :, SLAB * (2 * kvh + 1):SLAB * (2 * kvh + 2)] = acc[tq:2 * tq]


def _tri(n, with_total=False):
    r = np.arange(n)
    t = (r[:, None] > r[None, :]).astype(np.float32)
    if with_total:
        t = np.concatenate([t, np.ones((n, 128), np.float32)], axis=1)
    return jnp.asarray(t, dtype=BF16)


def _sb_prompt(q, kvb):
    nb, t, _ = q.shape
    tq = TILE
    return pl.pallas_call(
        _sb_prompt_kernel, grid=(nb, t // tq),
        in_specs=[pl.BlockSpec((1, tq, 512), lambda n, i: (n, i, 0)),
                  pl.BlockSpec((1, t, 256), lambda n, i: (n, 0, 0)),
                  pl.BlockSpec((tq, tq + 128), lambda n, i: (0, 0))],
        out_specs=pl.BlockSpec((1, tq, 512), lambda n, i: (n, i, 0)),
        out_shape=jax.ShapeDtypeStruct((nb, t, 512), BF16),
        scratch_shapes=[pltpu.VMEM((2 * tq, SLAB), F32), pltpu.VMEM((2 * tq, 128), F32)],
        compiler_params=_cparams(("parallel", "arbitrary")),
        name="sb_prompt",
    )(q, kvb, _tri(tq, with_total=True))


def _softmax_step(s, vt, m_ref, l_ref, acc_ref, first, v_is_transposed=False):
    pv = _dot_nt if v_is_transposed else _dot
    if first:
        m_new = jnp.max(s, axis=1, keepdims=True)
        p = jnp.exp(s - m_new)
        l_ref[...] = jnp.sum(p, axis=1, keepdims=True)
        acc_ref[...] = pv(p.astype(BF16), vt)
    else:
        m_old = m_ref[...]
        m_new = jnp.maximum(m_old, jnp.max(s, axis=1, keepdims=True))
        a = jnp.exp(m_old - m_new)
        p = jnp.exp(s - m_new)
        l_ref[...] = a * l_ref[...] + jnp.sum(p, axis=1, keepdims=True)
        acc_ref[...] = a * acc_ref[...] + pv(p.astype(BF16), vt)
    m_ref[...] = m_new


def _lane_max(s):
    return jnp.maximum(s[:, 0:128], s[:, 128:256])


def _row_max_rep(mrun):
    return jnp.broadcast_to(jnp.max(mrun, axis=1, keepdims=True), mrun.shape)


def _mla_prompt_kernel(q_ref, rows_ref, wuv_ref, o_ref, m_ref, acc_ref):
    tq = q_ref.shape[1]
    i = pl.program_id(1)
    q4 = jnp.concatenate([q_ref[0, :, 256 * h:256 * (h + 1)] for h in range(N_HEADS)], axis=0)
    rowq = lax.broadcasted_iota(jnp.int32, (N_HEADS * tq, tq), 0) & (tq - 1)
    col = lax.broadcasted_iota(jnp.int32, (N_HEADS * tq, tq), 1)

    def rows_tile(j):
        return rows_ref[0, pl.ds(pl.multiple_of(j * tq, tq), tq), :]

    rt = rows_tile(i)
    s_diag = jnp.where(col <= rowq, _dot_nt(q4, rt), NEG)
    m_ref[...] = _lane_max(s_diag)

    def max_body(k, _):
        m_ref[...] = jnp.maximum(m_ref[...], _lane_max(_dot_nt(q4, rows_tile(i - 1 - k))))
        return 0

    lax.fori_loop(0, i, max_body, 0)
    m_ref[...] = _row_max_rep(m_ref[...])
    acc_ref[...] = _dot(jnp.exp(s_diag - _rep2(m_ref[...])).astype(BF16), rt)

    def body(k, _):
        rt = rows_tile(i - 1 - k)
        acc_ref[...] += _dot(jnp.exp(_dot_nt(q4, rt) - _rep2(m_ref[...])).astype(BF16), rt)
        return 0

    lax.fori_loop(0, i, body, 0)
    olat = (acc_ref[:, 0:KV_LORA] / jnp.maximum(acc_ref[:, 255:256], 1e-30)).astype(BF16)
    for h in range(N_HEADS):
        o_ref[0, :, SLAB * h:SLAB * (h + 1)] = _dot(olat[h * tq:(h + 1) * tq], wuv_ref[h]).astype(BF16)


def _mla_prompt(q, rowsb, w_uv):
    nb, t, _ = q.shape
    tq = TILE
    return pl.pallas_call(
        _mla_prompt_kernel, grid=(nb, t // tq),
        in_specs=[pl.BlockSpec((1, tq, 1024), lambda n, i: (n, i, 0)),
                  pl.BlockSpec((1, t, 256), lambda n, i: (n, 0, 0)),
                  pl.BlockSpec((N_HEADS, 128, 128), lambda n, i: (0, 0, 0))],
        out_specs=pl.BlockSpec((1, tq, 512), lambda n, i: (n, i, 0)),
        out_shape=jax.ShapeDtypeStruct((nb, t, 512), BF16),
        scratch_shapes=[pltpu.VMEM((N_HEADS * tq, 128), F32), pltpu.VMEM((N_HEADS * tq, 256), F32)],
        compiler_params=_cparams(("parallel", "arbitrary")),
        name="mla_prompt",
    )(q, rowsb, w_uv)


def _t5_bucket(n):
    exact = NUM_BUCKETS // 2
    nf = jnp.maximum(n, 1).astype(F32)
    far = exact + (jnp.log(nf / exact) / math.log(MAX_DISTANCE / exact) * (NUM_BUCKETS - exact)).astype(jnp.int32)
    return jnp.where(n < exact, n, jnp.minimum(far, NUM_BUCKETS - 1))


def _bias_kernel(tab_ref, lut_ref, tile_ref):
    h = pl.program_id(0)
    t = tile_ref.shape[2]

    def lookup(n):
        bucket = _t5_bucket(jnp.maximum(n, 0))
        out = jnp.zeros(n.shape, F32)
        for b in range(NUM_BUCKETS):
            out = jnp.where(bucket == b, tab_ref[b, h], out)
        return out

    lut_ref[0] = lookup(lax.broadcasted_iota(jnp.int32, (8, 128), 1))
    rel = lax.broadcasted_iota(jnp.int32, (t, t), 0) - lax.broadcasted_iota(jnp.int32, (t, t), 1)
    tile_ref[0, 0] = lookup(rel)
    tile_ref[0, 1] = lookup(rel + t)


def _bias_tables(rel_bias):
    nh = rel_bias.shape[1]
    lut, tiles = pl.pallas_call(
        _bias_kernel, grid=(nh,),
        in_specs=[pl.BlockSpec(memory_space=pltpu.SMEM)],
        out_specs=[pl.BlockSpec((1, 8, 128), lambda h: (h, 0, 0)),
                   pl.BlockSpec((1, 2, TILE, TILE), lambda h: (h, 0, 0, 0))],
        out_shape=[jax.ShapeDtypeStruct((nh, 8, 128), F32), jax.ShapeDtypeStruct((nh, 2, TILE, TILE), F32)],
        compiler_params=_cparams(("arbitrary",)),
        name="bias_tables",
    )(rel_bias)
    return lut[:, 0, :], tiles


def _group_tiles(tiles, far, groups):
    h = tiles.shape[0]
    g = h // groups
    t = tiles.shape[2]
    bt = tiles.reshape(groups, g, 2, t, t).transpose(0, 2, 1, 3, 4).reshape(groups, 2, g * t, t)
    fr = jnp.broadcast_to(far.reshape(groups, g, 1, 1), (groups, g, t, 1)).reshape(groups, g * t, 1)
    return bt, fr


def _topk_ids(score, k):
    col = lax.broadcasted_iota(jnp.int32, score.shape, 1)
    ids = []
    for _ in range(k):
        m = jnp.max(score, axis=1, keepdims=True)
        first = jnp.min(jnp.where(score == m, col, 1 << 20), axis=1, keepdims=True)
        ids.append(jnp.where(m > 0.5 * NEG, first, -1))
        score = jnp.where(col == first, NEG, score)
    return ids


def _topk_ids_t(score_t, k):
    row = lax.broadcasted_iota(jnp.int32, score_t.shape, 0)
    ids = []
    for _ in range(k):
        m = jnp.max(score_t, axis=0, keepdims=True)
        first = jnp.min(jnp.where(score_t == m, row, 1 << 20), axis=0, keepdims=True)
        ids.append(jnp.where(m > 0.5 * NEG, first, -1))
        score_t = jnp.where(row == first, NEG, score_t)
    return ids


def _with_ones(kvt):
    return jnp.concatenate([kvt, jnp.ones(kvt.shape, kvt.dtype)], axis=1)


def _where_rep(chosen, s, other):
    return jnp.concatenate([jnp.where(chosen, s[:, 0:128], other), jnp.where(chosen, s[:, 128:256], other)], axis=1)


def _moba_prompt_kernel(q_ref, kv_ref, kbar_ref, bt_ref, far_ref, o_ref, m_ref, acc_ref, id_ref):
    tq = q_ref.shape[1]
    i = pl.program_id(1)
    rowq = lax.broadcasted_iota(jnp.int32, (2 * tq, tq), 0) & (tq - 1)
    col = lax.broadcasted_iota(jnp.int32, (2 * tq, tq), 1)
    brow = lax.broadcasted_iota(jnp.int32, (kbar_ref.shape[1], 2 * tq), 0)
    for kvh in range(2):
        q2 = jnp.concatenate([q_ref[0, :, SLAB * (2 * kvh):SLAB * (2 * kvh + 1)],
                              q_ref[0, :, SLAB * (2 * kvh + 1):SLAB * (2 * kvh + 2)]], axis=0)
        kb = kbar_ref[0, :, SLAB * kvh:SLAB * (kvh + 1)]
        kb_hi = kb.astype(BF16)
        kb_lo = (kb - kb_hi.astype(F32)).astype(BF16)
        gs_t = _dot_nt(kb_hi, q2) + _dot_nt(kb_lo, q2)
        ids = _topk_ids_t(jnp.where(brow < i, gs_t, NEG), MOBA_TOP)
        for r in range(MOBA_TOP):
            id_ref[r] = jnp.broadcast_to(ids[r].astype(F32), (128, 2 * tq)).T
        far2 = jnp.broadcast_to(far_ref[kvh], (2 * tq, tq))

        def kv_tile(j):
            return kv_ref[0, pl.ds(pl.multiple_of(j * tq, tq), tq), SLAB * kvh:SLAB * (kvh + 1)]

        def past_scores(k):
            j = i - 1 - k
            kvt = kv_tile(j)
            jf = j.astype(F32)
            chosen = (id_ref[0] == jf) | (id_ref[1] == jf) | (id_ref[2] == jf)
            bias = jnp.where(k == 0, bt_ref[kvh, 1], far2)
            return _where_rep(chosen, _dot_nt(q2, kvt) + bias, NEG), kvt

        kvt = kv_tile(i)
        s_diag = jnp.where(col <= rowq, _dot_nt(q2, kvt) + bt_ref[kvh, 0], NEG)
        m_ref[...] = _lane_max(s_diag)

        def max_body(k, _):
            m_ref[...] = jnp.maximum(m_ref[...], _lane_max(past_scores(k)[0]))
            return 0

        lax.fori_loop(0, i, max_body, 0)
        m_ref[...] = _row_max_rep(m_ref[...])
        acc_ref[...] = _dot(jnp.exp(s_diag - _rep2(m_ref[...])).astype(BF16), _with_ones(kvt))

        def body(k, _):
            s, kvt = past_scores(k)
            acc_ref[...] += _dot(jnp.exp(s - _rep2(m_ref[...])).astype(BF16), _with_ones(kvt))
            return 0

        lax.fori_loop(0, i, body, 0)
        out = (acc_ref[:, 0:SLAB] / jnp.maximum(acc_ref[:, SLAB:2 * SLAB], 1e-30)).astype(BF16)
        o_ref[0, :, SLAB * (2 * kvh):SLAB * (2 * kvh + 1)] = out[0:tq]
        o_ref[0, :, SLAB * (2 * kvh + 1):SLAB * (2 * kvh + 2)] = out[tq:2 * tq]


def _moba_prompt(q, kvb, kbar, bt, far):
    nb, t, _ = q.shape
    tq = TILE
    assert tq == MOBA_BLOCK
    full = lambda a: pl.BlockSpec(a.shape, lambda n, i: (0,) * a.ndim)
    return pl.pallas_call(
        _moba_prompt_kernel, grid=(nb, t // tq),
        in_specs=[pl.BlockSpec((1, tq, 512), lambda n, i: (n, i, 0)),
                  pl.BlockSpec((1, t, 256), lambda n, i: (n, 0, 0)),
                  pl.BlockSpec((1,) + kbar.shape[1:], lambda n, i: (n, 0, 0)),
                  full(bt), full(far)],
        out_specs=pl.BlockSpec((1, tq, 512), lambda n, i: (n, i, 0)),
        out_shape=jax.ShapeDtypeStruct((nb, t, 512), BF16),
        scratch_shapes=[pltpu.VMEM((2 * tq, 128), F32), pltpu.VMEM((2 * tq, 2 * SLAB), F32),
                        pltpu.VMEM((MOBA_TOP, 2 * tq, 128), F32)],
        compiler_params=_cparams(("parallel", "arbitrary")),
        name="moba_prompt",
    )(q, kvb, kbar, bt, far)


def _cmp_a_kernel(x_ref, w_ref, o_ref):
    o_ref[0] = _dot(x_ref[0].astype(BF16), w_ref[...])


def _cmp_a(x, w_c1):
    nb, j1, c = x.shape
    tj = min(TILE, j1)
    return pl.pallas_call(
        _cmp_a_kernel, grid=(nb, j1 // tj),
        in_specs=[pl.BlockSpec((1, tj, c), lambda n, i: (n, i, 0)),
                  pl.BlockSpec(w_c1.shape, lambda n, i: (0, 0), pipeline_mode=pl.Buffered(1))],
        out_specs=pl.BlockSpec((1, tj, 512), lambda n, i: (n, i, 0)),
        out_shape=jax.ShapeDtypeStruct((nb, j1, 512), F32),
        compiler_params=_cparams(("parallel", "parallel")),
        name="cmp_a",
    )(x, w_c1)


def _gelu_tanh(x):
    return 0.5 * x * (1.0 + jnp.tanh(math.sqrt(2.0 / math.pi) * (x + 0.044715 * (x * x * x))))


def _cmp_b_kernel(fs_ref, pe_ref, w1_ref, w2_ref, o_ref):
    j1 = fs_ref.shape[1]
    pe = _dot(pe_ref[...].astype(BF16), w1_ref[...])
    pe_term = pe[0:1, 0:256] + pe[1:2, 256:512]
    first = fs_ref[0, :, 0:256]
    nxt = pltpu.roll(fs_ref[0, :, 256:512], j1 - 1, 0)
    h = _gelu_tanh(first + nxt + pe_term)
    o_ref[0] = _dot(h.astype(BF16), w2_ref[...]).astype(BF16)


def _cmp_b(fs, pe_rows, w_c1, w_c2):
    nb, j1, _ = fs.shape
    full = lambda a: pl.BlockSpec(a.shape, lambda n: (0,) * a.ndim)
    return pl.pallas_call(
        _cmp_b_kernel, grid=(nb,),
        in_specs=[pl.BlockSpec((1, j1, 512), lambda n: (n, 0, 0)), full(pe_rows), full(w_c1), full(w_c2)],
        out_specs=pl.BlockSpec((1, j1, SLAB), lambda n: (n, 0, 0)),
        out_shape=jax.ShapeDtypeStruct((nb, j1, SLAB), BF16),
        compiler_params=_cparams(("parallel",)),
        name="cmp_b",
    )(fs, pe_rows, w_c1, w_c2)


def _lut_gather(lut_row, idx):
    m, c = idx.shape
    src = jnp.broadcast_to(lut_row, (m, 128))
    return jnp.concatenate([jnp.take_along_axis(src, idx[:, 128 * k:128 * (k + 1)], axis=1)
                            for k in range(c // 128)], axis=1)


def _masked_softmax(s, valid):
    m = jnp.max(s, axis=1, keepdims=True)
    e = jnp.where(valid, jnp.exp(s - m), 0.0)
    return e / jnp.maximum(jnp.sum(e, axis=1, keepdims=True), 1e-30)


def _select_blocks(imp, qpos):
    b = lax.broadcasted_iota(jnp.int32, imp.shape, 1)
    cur = qpos >> 6
    forced = (b == 0) | (b == cur) | (b == cur - 1)
    score = jnp.where(forced, -NEG, jnp.where(b <= cur, imp, NEG))
    sel = jnp.zeros(imp.shape, F32)
    for _ in range(SEL_TOP):
        m = jnp.max(score, axis=1, keepdims=True)
        first = jnp.min(jnp.where(score == m, b, 1 << 20), axis=1, keepdims=True)
        hit = (b == first) & (m > 0.5 * NEG)
        sel = jnp.where(hit, 1.0, sel)
        score = jnp.where(hit, NEG, score)
    return sel


def _select_blocks_t(imp_t, qpos):
    b = lax.broadcasted_iota(jnp.int32, imp_t.shape, 0)
    cur = qpos >> 6
    forced = (b == 0) | (b == cur) | (b == cur - 1)
    score = jnp.where(forced, -NEG, jnp.where(b <= cur, imp_t, NEG))
    sel = jnp.zeros(imp_t.shape, F32)
    for _ in range(SEL_TOP):
        m = jnp.max(score, axis=0, keepdims=True)
        first = jnp.min(jnp.where(score == m, b, 1 << 20), axis=0, keepdims=True)
        hit = (b == first) & (m > 0.5 * NEG)
        sel = jnp.where(hit, 1.0, sel)
        score = jnp.where(hit, NEG, score)
    return sel


def _nsa_prompt_kernel(q_ref, kv_ref, cmp_ref, g_ref, lut_ref, cover_ref, e_ref, bt_ref, far_ref, o_ref,
                       oc_ref, os_ref, mx_ref, acc2_ref):
    tq = q_ref.shape[1]
    j1 = cmp_ref.shape[1]
    i = pl.program_id(1)
    nh = N_HEADS
    q4 = jnp.concatenate([q_ref[0, :, SLAB * h:SLAB * (h + 1)] for h in range(nh)], axis=0)
    rowq = lax.broadcasted_iota(jnp.int32, (tq, tq), 0)
    col = lax.broadcasted_iota(jnp.int32, (tq, tq), 1)
    rep = lambda x: jnp.concatenate([x] * nh, axis=0)
    qpos = i * tq + lax.broadcasted_iota(jnp.int32, (tq, 1), 0)

    cmp = cmp_ref[0]
    jcol = lax.broadcasted_iota(jnp.int32, (tq, j1), 1)
    rel_c = qpos - (jcol * CMP_STRIDE + (CMP_LEN - 1))
    valid = (rel_c >= 0) & (jcol < j1 - 1)
    idx = jnp.clip(rel_c, 0, 127)
    bias = jnp.concatenate([_lut_gather(lut_ref[h:h + 1, :], idx) for h in range(nh)], axis=0)
    valid4 = rep(valid)
    pc = _masked_softmax(jnp.where(valid4, _dot_nt(q4, cmp) + bias, NEG), valid4)
    oc_ref[...] = _dot(pc.astype(BF16), cmp)
    psum = pc[0:tq] + pc[tq:2 * tq] + pc[2 * tq:3 * tq] + pc[3 * tq:4 * tq]
    p_hi = psum.astype(BF16)
    p_lo = (psum - p_hi.astype(F32)).astype(BF16)
    imp_t = _dot_nt(cover_ref[...], p_hi) + _dot_nt(cover_ref[...], p_lo)
    qpos_row = i * tq + lax.broadcasted_iota(jnp.int32, (1, tq), 1)
    selb = _select_blocks_t(imp_t, qpos_row).T.astype(BF16)

    def kv_tile(j, part):
        return kv_ref[0, pl.ds(pl.multiple_of(j * tq, tq), tq), SLAB * part:SLAB * (part + 1)]

    far2 = jnp.broadcast_to(far_ref[...], (nh * tq, tq))

    def past_scores(k):
        j = i - 1 - k
        kvt = kv_tile(j, 1)
        bias = jnp.where(k == 0, bt_ref[1], far2)
        ok = _dot(selb, e_ref[j]) > 0.5
        return jnp.where(rep(ok), _dot_nt(q4, kvt) + bias, NEG), kvt

    kvt = kv_tile(i, 1)
    ok = (_dot(selb, e_ref[i]) > 0.5) & (col <= rowq)
    s_diag = jnp.where(rep(ok), _dot_nt(q4, kvt) + bt_ref[0], NEG)
    mx_ref[...] = _lane_max(s_diag)

    def max_body(k, _):
        mx_ref[...] = jnp.maximum(mx_ref[...], _lane_max(past_scores(k)[0]))
        return 0

    lax.fori_loop(0, i, max_body, 0)
    mx_ref[...] = _row_max_rep(mx_ref[...])
    acc2_ref[...] = _dot(jnp.exp(s_diag - _rep2(mx_ref[...])).astype(BF16), _with_ones(kvt))

    def body(k, _):
        s, kvt = past_scores(k)
        acc2_ref[...] += _dot(jnp.exp(s - _rep2(mx_ref[...])).astype(BF16), _with_ones(kvt))
        return 0

    lax.fori_loop(0, i, body, 0)
    os_ref[...] = acc2_ref[:, 0:SLAB] / jnp.maximum(acc2_ref[:, SLAB:2 * SLAB], 1e-30)

    kv0, kv1, kv2 = kv_tile(i, 2), kv_tile(jnp.maximum(i - 1, 0), 2), kv_tile(jnp.maximum(i - 2, 0), 2)
    s0 = jnp.where(rep(col <= rowq), _dot_nt(q4, kv0) + bt_ref[0], NEG)
    s1 = jnp.where(i >= 1, _dot_nt(q4, kv1) + bt_ref[1], NEG)
    s2 = jnp.where(rep(col > rowq) & (i >= 2), _dot_nt(q4, kv2) + far2, NEG)
    mw = _row_max_rep(jnp.maximum(jnp.maximum(_lane_max(s0), _lane_max(s1)), _lane_max(s2)))
    mw2 = _rep2(mw)
    accw = (_dot(jnp.exp(s0 - mw2).astype(BF16), _with_ones(kv0))
            + _dot(jnp.exp(s1 - mw2).astype(BF16), _with_ones(kv1))
            + _dot(jnp.exp(s2 - mw2).astype(BF16), _with_ones(kv2)))
    ow = accw[:, 0:SLAB] / jnp.maximum(accw[:, SLAB:2 * SLAB], 1e-30)
    g = g_ref[0]
    for h in range(nh):
        r = slice(h * tq, (h + 1) * tq)
        o = (g[:, h:h + 1] * oc_ref[r, :] + g[:, nh + h:nh + h + 1] * os_ref[r, :]
             + g[:, 2 * nh + h:2 * nh + h + 1] * ow[r])
        o_ref[0, :, SLAB * h:SLAB * (h + 1)] = o.astype(BF16)


def _cover(j1, nb_pad):
    j = np.arange(j1)[:, None]
    b = np.arange(nb_pad)[None, :]
    cstart, cend, bstart = j * CMP_STRIDE, j * CMP_STRIDE + CMP_LEN - 1, b * SEL_BLOCK
    return jnp.asarray(((cstart < bstart + SEL_BLOCK) & (cend >= bstart)).astype(np.float32), dtype=BF16)


def _expand(n_tiles, tk, nb_pad):
    key = np.arange(n_tiles * tk).reshape(n_tiles, 1, tk)
    b = np.arange(nb_pad).reshape(1, nb_pad, 1)
    return jnp.asarray((key // SEL_BLOCK == b).astype(np.float32), dtype=BF16)


def _nsa_prompt(q, kvb, cmp, gates, lut, bt, far):
    nb, t, _ = q.shape
    tq = TILE
    assert WINDOW == 2 * tq and t // SEL_BLOCK <= 128 and t // SEL_BLOCK >= SEL_TOP
    j1 = cmp.shape[1]
    cover = _cover(j1, 128).T
    e = _expand(t // tq, tq, 128)
    full = lambda a: pl.BlockSpec(a.shape, lambda n, i: (0,) * a.ndim)
    return pl.pallas_call(
        _nsa_prompt_kernel, grid=(nb, t // tq),
        in_specs=[pl.BlockSpec((1, tq, 512), lambda n, i: (n, i, 0)),
                  pl.BlockSpec((1, t, 384), lambda n, i: (n, 0, 0)),
                  pl.BlockSpec((1, j1, SLAB), lambda n, i: (n, 0, 0)),
                  pl.BlockSpec((1, tq, 128), lambda n, i: (n, i, 0)),
                  full(lut), full(cover), full(e), full(bt), full(far)],
        out_specs=pl.BlockSpec((1, tq, 512), lambda n, i: (n, i, 0)),
        out_shape=jax.ShapeDtypeStruct((nb, t, 512), BF16),
        scratch_shapes=[pltpu.VMEM((N_HEADS * tq, SLAB), F32), pltpu.VMEM((N_HEADS * tq, SLAB), F32),
                        pltpu.VMEM((N_HEADS * tq, 128), F32), pltpu.VMEM((N_HEADS * tq, 2 * SLAB), F32)],
        compiler_params=_cparams(("parallel", "arbitrary")),
        name="nsa_prompt",
    )(q, kvb, cmp, gates, lut, cover, e, bt, far)


def kernel(x_prompt, x_sample, cache_sb_kv, cache_nsa_kv, state_nsa_win, cache_mla, cache_moba_kv, page_table, c_prompt, c_sample, rel_bias, ln_in_g, ln_in_b, w_ada, b_ada, w_in, nsa_cmp_pe, nsa_cmp_w1, nsa_cmp_w2, mla_q_norm, mla_kv_norm, mla_w_uq, mla_w_uk, mla_w_uv, w_br, w_o, ln1_g, ln1_b, w_ff1, w_ff3, w_ff2, ln2_g, ln2_b):
    w = dict(w_ada=w_ada, b_ada=b_ada, w_in=w_in, nsa_cmp_pe=nsa_cmp_pe, nsa_cmp_w1=nsa_cmp_w1, nsa_cmp_w2=nsa_cmp_w2,
             mla_q_norm=mla_q_norm, mla_kv_norm=mla_kv_norm, mla_w_uq=mla_w_uq, mla_w_uk=mla_w_uk,
             mla_w_uv=mla_w_uv, w_br=w_br, w_o=w_o, ln1_g=ln1_g, ln1_b=ln1_b, w_ff1=w_ff1, w_ff3=w_ff3,
             w_ff2=w_ff2, ln2_g=ln2_g, ln2_b=ln2_b, ln_in_g=ln_in_g, ln_in_b=ln_in_b, rel_bias=rel_bias)
    packs = [_pack_layer(l, w_in, mla_w_uq, mla_w_uk, mla_w_uv, w_br, nsa_cmp_pe, nsa_cmp_w1, nsa_cmp_w2)
             for l in range(DEPTH)]
    tabs = _bias_tables(rel_bias)
    y_p, st_p = _prompt_trunk(x_prompt, c_prompt, w, packs, tabs)
    pools = (cache_sb_kv, cache_nsa_kv, state_nsa_win, cache_mla, cache_moba_kv)
    y_s, st_s = _decode_trunk(x_sample, c_sample, pools, page_table, w, packs, tabs)
    out = [y_p, y_s]
    for a, b in zip(st_p, st_s):
        out += [a, b]
    return tuple(out)


def _decode_trunk(x, c, pools, page_table, w, packs, tabs):
    n_seq, tt, _ = x.shape
    assert tt <= 8 and tt & (tt - 1) == 0
    sb_pool, nsa_pool, win_state, mla_pool, moba_pool = pools
    n_pool = sb_pool.shape[1]
    n_pages = page_table.shape[1]
    past_len = n_pages * PAGE
    assert past_len % MOBA_BLOCK == 0 and past_len >= WINDOW and past_len // SEL_BLOCK + 1 <= 256
    n_chunks = n_pages // G_PAGES
    rows = 8 * N_HEADS
    lut, _ = tabs
    far_nsa = jnp.repeat(w["rel_bias"][NUM_BUCKETS - 1, 0:N_HEADS], 8).reshape(rows, 1)
    far_moba = jnp.repeat(w["rel_bias"][NUM_BUCKETS - 1, N_HEADS:2 * N_HEADS], 8).reshape(rows, 1)
    feat_major = lambda p: jnp.transpose(p, (0, 1, 3, 4, 5, 2)).reshape(DEPTH, n_pool, 256, PAGE)
    sb_pool = feat_major(sb_pool)
    nsa_pool4 = feat_major(nsa_pool)
    moba_pool = feat_major(moba_pool)
    mla_pool = jnp.transpose(mla_pool, (0, 1, 3, 2))
    win_state = win_state.reshape(DEPTH, n_seq, win_state.shape[2], SLAB)
    r = n_seq * tt
    xf = x.reshape(1, r, D_MODEL)
    per_seq = lambda a: a.reshape(n_seq, tt, a.shape[-1])
    new = ([], [], [], [], [])
    for l in range(DEPTH):
        pk = packs[l]
        mod = _ada(c, w["w_ada"][l], w["b_ada"][l])
        mod = jnp.repeat(mod, tt, axis=0).reshape(1, r, 6 * D_MODEL)
        res = _proj(xf, mod, w["ln_in_g"], w["ln_in_b"], pk, w["mla_q_norm"][l], w["mla_kv_norm"][l],
                    apply_ln=(l == 0), pos0=past_len, period=tt, mod_per_row=True, emit_kbar=False)
        (sbq, sbkv, sbkvb, nsaq, nsakv, nsawin, nsakvb, nsag, mlar, mlarb, mlaq, mobaq, mobakv, mobakvb) = res[:14]
        if l == 0:
            xf = res[14]
        o_sb = _sb_dec(page_table, sb_pool, l, per_seq(sbq), per_seq(sbkvb))
        o_mla = _mla_dec(page_table, mla_pool, l, per_seq(mlaq), per_seq(mlarb), pk["w_uv"])
        o_moba = _moba_dec(page_table, moba_pool, l, per_seq(mobaq), per_seq(mobakvb), lut, far_moba)
        fs = _cmp_a_dec(page_table, nsa_pool4, l, pk["w_c1d"])
        cmp = _cmp_b(fs, pk["pe_rows"], pk["w_c1"], pk["w_c2"])
        o_nsa = _nsa_dec(page_table, nsa_pool4, l, per_seq(nsaq), per_seq(nsakvb), cmp, per_seq(nsag), win_state[l],
                         lut, far_nsa)
        flat = lambda o: o.reshape(1, r, 512)
        xf = _dense_tail(l, xf, mod, True, (flat(o_sb), flat(o_nsa), flat(o_mla), flat(o_moba)), w, pk)
        wl = win_state.shape[2]
        win_new = jnp.concatenate([win_state[l], per_seq(nsawin)], axis=1)[:, tt:]
        rows_out = (sbkv.reshape(n_seq, tt, 2, 2, HD), nsakv.reshape(n_seq, tt, 2, 2, HD),
                    win_new.reshape(n_seq, wl, 2, HD), per_seq(mlar), mobakv.reshape(n_seq, tt, 2, 2, HD))
        for lst, a in zip(new, rows_out):
            lst.append(a)
    return xf.reshape(n_seq, tt, D_MODEL), [jnp.stack(lst, axis=0) for lst in new]


G_PAGES = 32


def _page_specs(l, feats, n_chunks, reverse):
    def spec(g):
        if reverse:
            return pl.BlockSpec((None, None, feats, PAGE),
                                lambda n, c, pt: (l, pt[n, (n_chunks - 1 - c) * G_PAGES + g], 0, 0))
        return pl.BlockSpec((None, None, feats, PAGE), lambda n, c, pt: (l, pt[n, c * G_PAGES + g], 0, 0))
    return [spec(g) for g in range(G_PAGES)]


def _seq_spec(shape):
    nd = len(shape)
    return pl.BlockSpec((1,) + tuple(shape[1:]), lambda n, c, pt: (n,) + (0,) * (nd - 1))


def _const_spec(a):
    return pl.BlockSpec(a.shape, lambda n, c, pt: (0,) * a.ndim)


def _paged_call(kernel, name, page_table, cache, l, lanes, reverse, seq_ins, const_ins, out_lanes, scratch):
    n_seq, n_pages = page_table.shape
    assert n_pages % G_PAGES == 0
    n_chunks = n_pages // G_PAGES
    tt = seq_ins[0].shape[1]
    grid_spec = pltpu.PrefetchScalarGridSpec(
        num_scalar_prefetch=1, grid=(n_seq, n_chunks),
        in_specs=[_seq_spec(a.shape) for a in seq_ins] + [_const_spec(a) for a in const_ins]
        + _page_specs(l, lanes, n_chunks, reverse),
        out_specs=_seq_spec((n_seq, tt, out_lanes)),
        scratch_shapes=scratch)
    return pl.pallas_call(
        functools.partial(kernel, n_seq_ins=len(seq_ins), n_const=len(const_ins)),
        grid_spec=grid_spec, out_shape=jax.ShapeDtypeStruct((n_seq, tt, out_lanes), F32),
        compiler_params=_cparams(("parallel", "arbitrary")), name=name,
    )(page_table, *seq_ins, *const_ins, *([cache] * G_PAGES))


def _fill_rows(dst_ref, src, width, lane_of_head):
    dst_ref[...] = jnp.zeros(dst_ref.shape, F32)
    tt = src.shape[0]
    for h in range(N_HEADS):
        lo = lane_of_head(h)
        dst_ref[8 * h:8 * h + tt, lo:lo + width] = src[:, width * h:width * (h + 1)].astype(F32)


def _fill_new(dst_ref, src):
    dst_ref[...] = jnp.zeros(dst_ref.shape, F32)
    dst_ref[0:src.shape[0], :] = src.astype(F32)


def _row_t(shape):
    return lax.broadcasted_iota(jnp.int32, shape, 0) & 7


def _head_lut(lut_ref, head0, idx):
    return jnp.concatenate([_lut_gather(lut_ref[head0 + h:head0 + h + 1, :], idx[8 * h:8 * h + 8])
                            for h in range(N_HEADS)], axis=0)


def _sb_dec_kernel(pt_ref, *refs, n_seq_ins, n_const):
    q_ref, new_ref, tri_ref = refs[:3]
    pages = refs[3:3 + G_PAGES]
    o_ref, q32_ref, new32_ref, acc_ref, carry_ref = refs[3 + G_PAGES:]
    c = pl.program_id(1)
    tt = q_ref.shape[1]
    tri = tri_ref[...]
    rt = _row_t((8 * N_HEADS, PAGE))
    col = lax.broadcasted_iota(jnp.int32, (8 * N_HEADS, PAGE), 1)

    @pl.when(c == 0)
    def _():
        _fill_rows(q32_ref, q_ref[0], SLAB, lambda h: SLAB * (h // 2))
        _fill_new(new32_ref, new_ref[0])
        kvt = new32_ref[...].astype(BF16)
        z = _dot_nt(q32_ref[...].astype(BF16), kvt)
        contrib, carry = _sb_tile(z, kvt, tri, jnp.zeros((8 * N_HEADS, 1), F32), col < rt)
        acc_ref[...] = contrib
        carry_ref[...] = carry

    q = q32_ref[...].astype(BF16)
    kv = [pages[g][...].astype(BF16) for g in range(G_PAGES)]
    zs = [_dot(q, kv[g]) for g in range(G_PAGES)]
    lrs = [_neg_softplus(z) for z in zs]
    tails = _split_dot(jnp.concatenate(lrs, axis=0), tri)
    carry = carry_ref[...]
    ws = [None] * G_PAGES
    for g in reversed(range(G_PAGES)):
        tail = tails[8 * N_HEADS * g:8 * N_HEADS * (g + 1)]
        ws[g] = jnp.exp(zs[g] + lrs[g] + tail + carry).astype(BF16)
        carry = carry + tail[:, 0:1] + lrs[g][:, 0:1]
    carry_ref[...] = carry
    acc_ref[...] += _dot_nt(jnp.concatenate(ws, axis=1), jnp.concatenate(kv, axis=1))

    @pl.when(c == pl.num_programs(1) - 1)
    def _():
        acc = acc_ref[...]
        for h in range(N_HEADS):
            o_ref[0, :, SLAB * h:SLAB * (h + 1)] = acc[8 * h:8 * h + tt, SLAB * (h // 2):SLAB * (h // 2 + 1)]


def _mla_dec_kernel(pt_ref, *refs, n_seq_ins, n_const):
    q_ref, new_ref, wuv_ref = refs[:3]
    pages = refs[3:3 + G_PAGES]
    o_ref, q32_ref, new32_ref, m_ref, l_ref, acc_ref = refs[3 + G_PAGES:]
    c = pl.program_id(1)
    tt = q_ref.shape[1]
    rt = _row_t((8 * N_HEADS, PAGE))
    col = lax.broadcasted_iota(jnp.int32, (8 * N_HEADS, PAGE), 1)

    @pl.when(c == 0)
    def _():
        _fill_rows(q32_ref, q_ref[0], 256, lambda h: 0)
        _fill_new(new32_ref, new_ref[0])
        rows = new32_ref[...].astype(BF16)
        s = jnp.where((col <= rt) & (col < tt), _dot_nt(q32_ref[...].astype(BF16), rows), NEG)
        _softmax_step(s, rows[:, 0:KV_LORA], m_ref, l_ref, acc_ref, True)

    q = q32_ref[...].astype(BF16)
    ss, vs = [], []
    for g in range(G_PAGES):
        lat = pages[g][0:KV_LORA, :].astype(BF16)
        ss.append(_dot(q[:, 0:KV_LORA], lat)
                  + _dot(q[:, KV_LORA:KV_LORA + ROPE], pages[g][KV_LORA:KV_LORA + ROPE, :].astype(BF16)))
        vs.append(lat)
    _softmax_step(jnp.concatenate(ss, axis=1), jnp.concatenate(vs, axis=1), m_ref, l_ref, acc_ref, False,
                  v_is_transposed=True)

    @pl.when(c == pl.num_programs(1) - 1)
    def _():
        olat = (acc_ref[...] / jnp.maximum(l_ref[...], 1e-30)).astype(BF16)
        for h in range(N_HEADS):
            o_ref[0, :, SLAB * h:SLAB * (h + 1)] = _dot(olat[8 * h:8 * h + 8], wuv_ref[h])[0:tt]


def _moba_dec_kernel(pt_ref, *refs, n_seq_ins, n_const, past_len):
    q_ref, new_ref, lut_ref, far_ref = refs[:4]
    pages = refs[4:4 + G_PAGES]
    (o_ref, q32_ref, new32_ref, gs_ref, mb_ref, lb_ref, accb_ref, blast_ref) = refs[4 + G_PAGES:]
    c = pl.program_id(1)
    n_chunks = gs_ref.shape[0]
    bpc = G_PAGES // 2
    tt = q_ref.shape[1]
    rows = 8 * N_HEADS
    rt = _row_t((rows, PAGE))
    col = lax.broadcasted_iota(jnp.int32, (rows, PAGE), 1)
    far = far_ref[...]

    @pl.when(c == 0)
    def _():
        _fill_rows(q32_ref, q_ref[0], SLAB, lambda h: SLAB * (h // 2))
        _fill_new(new32_ref, new_ref[0])
        gs_ref[...] = jnp.zeros(gs_ref.shape, F32)
        mb_ref[...] = jnp.zeros(mb_ref.shape, F32)
        lb_ref[...] = jnp.zeros(lb_ref.shape, F32)
        t2 = _row_t((rows, MOBA_BLOCK))
        s2 = lax.broadcasted_iota(jnp.int32, (rows, MOBA_BLOCK), 1)
        blast_ref[...] = _head_lut(lut_ref, N_HEADS, jnp.clip(MOBA_BLOCK + t2 - s2, 0, 127))

    q = q32_ref[...].astype(BF16)
    for k in range(bpc):
        kv = jnp.concatenate([pages[2 * k][...], pages[2 * k + 1][...]], axis=1).astype(BF16)
        is_last = (c == n_chunks - 1) & (k == bpc - 1)
        raw = _dot(q, kv)
        s = raw + jnp.where(is_last, blast_ref[...], far)
        m = jnp.max(s, axis=1, keepdims=True)
        p = jnp.exp(s - m)
        gs_ref[c, :, k:k + 1] = jnp.sum(raw, axis=1, keepdims=True) * (1.0 / MOBA_BLOCK)
        mb_ref[c, :, k:k + 1] = m
        lb_ref[c, :, k:k + 1] = jnp.sum(p, axis=1, keepdims=True)
        accb_ref[c * bpc + k] = _dot_nt(p.astype(BF16), kv)

    @pl.when(c == n_chunks - 1)
    def _():
        lane = lax.broadcasted_iota(jnp.int32, (rows, 128), 1)

        def gather(ref):
            out = jnp.zeros((rows, 128), F32)
            for cc in range(n_chunks):
                part = jnp.where(lane < bpc, ref[cc], 0.0)
                out = out + (part if cc == 0 else pltpu.roll(part, cc * bpc, 1))
            return out

        nblk = n_chunks * bpc
        gs = jnp.where(lane < nblk, gather(gs_ref), NEG)
        ids = _topk_ids(gs, MOBA_TOP)
        sel = (lane == ids[0]) | (lane == ids[1]) | (lane == ids[2])
        m_all = gather(mb_ref)
        l_all = gather(lb_ref)
        kvn = new32_ref[...].astype(BF16)
        bias_n = _head_lut(lut_ref, N_HEADS, jnp.clip(rt - col, 0, 127))
        s_n = jnp.where((col <= rt) & (col < tt), _dot_nt(q, kvn) + bias_n, NEG)
        m_n = jnp.max(s_n, axis=1, keepdims=True)
        m_fin = jnp.maximum(m_n, jnp.max(jnp.where(sel, m_all, NEG), axis=1, keepdims=True))
        p_n = jnp.exp(s_n - m_fin)
        wgt = jnp.where(sel, jnp.exp(m_all - m_fin), 0.0)
        l_fin = jnp.sum(p_n, axis=1, keepdims=True) + jnp.sum(wgt * l_all, axis=1, keepdims=True)
        acc = _dot(p_n.astype(BF16), kvn)
        for b in range(nblk):
            acc = acc + wgt[:, b:b + 1] * accb_ref[b]
        out = acc / jnp.maximum(l_fin, 1e-30)
        for h in range(N_HEADS):
            o_ref[0, :, SLAB * h:SLAB * (h + 1)] = out[8 * h:8 * h + tt, SLAB * (h // 2):SLAB * (h // 2 + 1)]


def _cmp_a_dec_kernel(pt_ref, *refs):
    w_ref = refs[0]
    pages = refs[1:1 + G_PAGES]
    o_ref, xt_ref = refs[1 + G_PAGES:]
    chunks = []
    for g in range(G_PAGES):
        xt_ref[...] = pages[g][...].T
        chunks.append(jnp.concatenate([xt_ref[pl.ds(s, PAGE // CMP_STRIDE, stride=CMP_STRIDE), :]
                                       for s in range(CMP_STRIDE)], axis=1))
    o_ref[0] = _dot(jnp.concatenate(chunks, axis=0).astype(BF16), w_ref[...])


def _cmp_a_dec(page_table, pool_t, l, w_c1d):
    n_seq, n_pages = page_table.shape
    n_chunks = n_pages // G_PAGES
    rows = PAGE // CMP_STRIDE
    grid_spec = pltpu.PrefetchScalarGridSpec(
        num_scalar_prefetch=1, grid=(n_seq, n_chunks),
        in_specs=[_const_spec(w_c1d)]
        + [pl.BlockSpec((None, None, SLAB, PAGE), lambda n, c, pt, g=g: (l, pt[n, c * G_PAGES + g], 0, 0))
           for g in range(G_PAGES)],
        out_specs=pl.BlockSpec((1, rows * G_PAGES, 512), lambda n, c, pt: (n, c, 0)),
        scratch_shapes=[pltpu.VMEM((PAGE, SLAB), F32)])
    return pl.pallas_call(
        _cmp_a_dec_kernel, grid_spec=grid_spec,
        out_shape=jax.ShapeDtypeStruct((n_seq, rows * n_pages, 512), F32),
        compiler_params=_cparams(("parallel", "arbitrary")), name="cmp_a_dec",
    )(page_table, w_c1d, *([pool_t] * G_PAGES))


def _nsa_dec_kernel(pt_ref, *refs, n_seq_ins, n_const, past_len):
    q_ref, new_ref, cmp_ref, g_ref, win_ref, lut_ref, cover_ref, far_ref = refs[:8]
    pages = refs[8:8 + G_PAGES]
    (o_ref, q32_ref, newsel_ref, newwin_ref, g32_ref, sel_ref, blast_ref, m_ref, l_ref, acc_ref,
     oc_ref, ow_ref) = refs[8 + G_PAGES:]
    c = pl.program_id(1)
    n_chunks = sel_ref.shape[0]
    n_pages = n_chunks * G_PAGES
    tt = q_ref.shape[1]
    nh = N_HEADS
    rows = 8 * nh
    rt = _row_t((rows, PAGE))
    col = lax.broadcasted_iota(jnp.int32, (rows, PAGE), 1)
    far = far_ref[...]
    rep = lambda x: jnp.concatenate([x] * nh, axis=0)

    @pl.when(c == 0)
    def _():
        _fill_rows(q32_ref, q_ref[0], SLAB, lambda h: 0)
        new = new_ref[0]
        _fill_new(newsel_ref, new[:, SLAB:2 * SLAB])
        _fill_new(newwin_ref, new[:, 2 * SLAB:3 * SLAB])
        g32_ref[...] = jnp.zeros(g32_ref.shape, F32)
        for h in range(nh):
            g32_ref[8 * h:8 * h + tt, :] = g_ref[0]
        q = q32_ref[...].astype(BF16)
        cmp = cmp_ref[0]
        j1 = cmp.shape[0]
        jcol = lax.broadcasted_iota(jnp.int32, (rows, j1), 1)
        rel_c = past_len + _row_t((rows, j1)) - (jcol * CMP_STRIDE + (CMP_LEN - 1))
        valid = jcol < j1 - 1
        bias = _head_lut(lut_ref, 0, jnp.clip(rel_c, 0, 127))
        pc = _masked_softmax(jnp.where(valid, _dot_nt(q, cmp) + bias, NEG), valid)
        oc_ref[...] = _dot(pc.astype(BF16), cmp)
        psum = pc[0:8] + pc[8:16] + pc[16:24] + pc[24:32]
        imp = _split_dot(psum, cover_ref[...])
        qpos = past_len + lax.broadcasted_iota(jnp.int32, (8, 1), 0)
        sel = _select_blocks(imp, qpos)
        per = 2 * G_PAGES
        sel_ref[...] = jnp.zeros(sel_ref.shape, F32)
        for cc in range(n_chunks):
            sel_ref[cc, :, 0:per] = sel[:, cc * per:(cc + 1) * per]
        blast_ref[...] = _head_lut(lut_ref, 0, jnp.clip(PAGE + rt - col, 0, 127))
        wst = win_ref[0].astype(BF16)
        wl = wst.shape[0]
        tw = _row_t((rows, wl))
        iw = lax.broadcasted_iota(jnp.int32, (rows, wl), 1)
        s = _dot_nt(q, wst) + _head_lut(lut_ref, 0, jnp.clip(wl + tw - iw, 0, 127))
        _softmax_step(jnp.where(iw > tw, s, NEG), wst, m_ref, l_ref, acc_ref, True)
        bias_n = _head_lut(lut_ref, 0, jnp.clip(rt - col, 0, 127))
        ok_n = (col <= rt) & (col < tt)
        wn = newwin_ref[...].astype(BF16)
        _softmax_step(jnp.where(ok_n, _dot_nt(q, wn) + bias_n, NEG), wn, m_ref, l_ref, acc_ref, False)
        ow_ref[...] = acc_ref[...] / jnp.maximum(l_ref[...], 1e-30)
        sn = newsel_ref[...].astype(BF16)
        _softmax_step(jnp.where(ok_n, _dot_nt(q, sn) + bias_n, NEG), sn, m_ref, l_ref, acc_ref, True)

    q = q32_ref[...].astype(BF16)
    selc = sel_ref[c]
    ss, vs = [], []
    for g in range(G_PAGES):
        kv = pages[g][SLAB:2 * SLAB, :].astype(BF16)
        chosen = jnp.where(col[0:8] < SEL_BLOCK, selc[:, 2 * g:2 * g + 1], selc[:, 2 * g + 1:2 * g + 2]) > 0.5
        is_last = (c == n_chunks - 1) & (g == G_PAGES - 1)
        s = _dot(q, kv) + jnp.where(is_last, blast_ref[...], far)
        ss.append(jnp.where(rep(chosen), s, NEG))
        vs.append(kv)
    _softmax_step(jnp.concatenate(ss, axis=1), jnp.concatenate(vs, axis=1), m_ref, l_ref, acc_ref, False,
                  v_is_transposed=True)

    @pl.when(c == n_chunks - 1)
    def _():
        osel = acc_ref[...] / jnp.maximum(l_ref[...], 1e-30)
        lane = lax.broadcasted_iota(jnp.int32, (rows, 128), 1)
        head = lax.broadcasted_iota(jnp.int32, (rows, 128), 0) >> 3
        g32 = g32_ref[...]
        gcol = lambda k: jnp.sum(jnp.where(lane == k * nh + head, g32, 0.0), axis=1, keepdims=True)
        out = gcol(0) * oc_ref[...] + gcol(1) * osel + gcol(2) * ow_ref[...]
        for h in range(nh):
            o_ref[0, :, SLAB * h:SLAB * (h + 1)] = out[8 * h:8 * h + tt]


_ROWS = 8 * N_HEADS


def _sb_dec(page_table, pool_t, l, q, new):
    vm = pltpu.VMEM
    return _paged_call(
        _sb_dec_kernel, "sb_dec", page_table, pool_t, l, 256, True, [q, new], [_tri(PAGE)], 512,
        [vm((_ROWS, 256), F32), vm((PAGE, 256), F32), vm((_ROWS, 256), F32), vm((_ROWS, 1), F32)])


def _mla_dec(page_table, pool_t, l, q, new, w_uv):
    vm = pltpu.VMEM
    return _paged_call(
        _mla_dec_kernel, "mla_dec", page_table, pool_t, l, KV_LORA + ROPE, False, [q, new], [w_uv], 512,
        [vm((_ROWS, 256), F32), vm((PAGE, 256), F32), vm((_ROWS, 1), F32), vm((_ROWS, 1), F32),
         vm((_ROWS, KV_LORA), F32)])


def _moba_dec(page_table, pool_t, l, q, new, lut, far):
    vm = pltpu.VMEM
    n_pages = page_table.shape[1]
    n_chunks = n_pages // G_PAGES
    return _paged_call(
        functools.partial(_moba_dec_kernel, past_len=n_pages * PAGE), "moba_dec", page_table, pool_t, l, 256, False,
        [q, new], [lut, far], 512,
        [vm((_ROWS, 256), F32), vm((PAGE, 256), F32)] + [vm((n_chunks, _ROWS, 128), F32)] * 3
        + [vm((n_pages // 2, _ROWS, 256), F32), vm((_ROWS, MOBA_BLOCK), F32)])


def _nsa_dec(page_table, pool_t, l, q, new, cmp, gates, win_state, lut, far):
    vm = pltpu.VMEM
    n_pages = page_table.shape[1]
    n_chunks = n_pages // G_PAGES
    past_len = n_pages * PAGE
    cover = _cover(past_len // CMP_STRIDE, 256)
    return _paged_call(
        functools.partial(_nsa_dec_kernel, past_len=past_len), "nsa_dec", page_table, pool_t, l, 256, False,
        [q, new, cmp, gates, win_state], [lut, cover, far], 512,
        [vm((_ROWS, SLAB), F32), vm((PAGE, SLAB), F32), vm((PAGE, SLAB), F32), vm((_ROWS, 128), F32),
         vm((n_chunks, 8, 128), F32), vm((_ROWS, PAGE), F32), vm((_ROWS, 1), F32), vm((_ROWS, 1), F32),
         vm((_ROWS, SLAB), F32), vm((_ROWS, SLAB), F32), vm((_ROWS, SLAB), F32)])


def _dense_tail(l, x, mod, mod_per_row, outs, w, pk):
    row = lambda a: a[l].reshape(1, -1)
    x = _row_call(_merge_kernel, "merge", x, mod, mod_per_row, list(outs),
                  [pk["w_gate"], pk["w_brp"], w["w_o"][l].astype(BF16), row(w["ln1_g"]), row(w["ln1_b"])])
    return _row_call(_ffn_kernel, "ffn", x, mod, mod_per_row, [],
                     [w["w_ff1"][l].astype(BF16), w["w_ff3"][l].astype(BF16), w["w_ff2"][l].astype(BF16),
                      row(w["ln2_g"]), row(w["ln2_b"])])


def _prompt_trunk(x, c, w, packs, tabs):
    nb, t, _ = x.shape
    lut, tiles = tabs
    bt_nsa, far_nsa = _group_tiles(tiles[0:4], w["rel_bias"][NUM_BUCKETS - 1, 0:4], 1)
    bt_moba, far_moba = _group_tiles(tiles[4:8], w["rel_bias"][NUM_BUCKETS - 1, 4:8], 2)
    new = ([], [], [], [], [])
    for l in range(DEPTH):
        pk = packs[l]
        mod = _ada(c, w["w_ada"][l], w["b_ada"][l]).reshape(nb, 1, 6 * D_MODEL)
        res = _proj(x, mod, w["ln_in_g"], w["ln_in_b"], pk, w["mla_q_norm"][l], w["mla_kv_norm"][l],
                    apply_ln=(l == 0), pos0=0, period=t, mod_per_row=False, emit_kbar=True)
        (sbq, sbkv, sbkvb, nsaq, nsakv, nsawin, nsakvb, nsag, mlar, mlarb, mlaq, mobaq, mobakv, mobakvb) = res[:14]
        if l == 0:
            x = res[14]
        kbar = res[-1].reshape(nb, t // TILE, 256)
        kbar = jnp.pad(kbar, ((0, 0), (0, 128 - kbar.shape[1]), (0, 0)))
        o_sb = _sb_prompt(sbq, sbkvb)
        fs = _cmp_a(nsakv.reshape(nb, t // CMP_STRIDE, CMP_STRIDE * 256), pk["w_c1"])
        cmp = _cmp_b(fs, pk["pe_rows"], pk["w_c1"], pk["w_c2"])
        o_nsa = _nsa_prompt(nsaq, nsakvb, cmp, nsag, lut[0:4], bt_nsa[0], far_nsa[0])
        o_mla = _mla_prompt(mlaq, mlarb, pk["w_uv"])
        o_moba = _moba_prompt(mobaq, mobakvb, kbar, bt_moba, far_moba)
        x = _dense_tail(l, x, mod, False, (o_sb, o_nsa, o_mla, o_moba), w, pk)
        wl = min(WINDOW, t)
        rows = (sbkv.reshape(nb, t, 2, 2, HD), nsakv.reshape(nb, t, 2, 2, HD),
                nsawin[:, t - wl:].reshape(nb, wl, 2, HD), mlar, mobakv.reshape(nb, t, 2, 2, HD))
        for lst, r in zip(new, rows):
            lst.append(r)
    return x, [jnp.stack(lst, axis=0) for lst in new]
```

```python
import functools
import math

import numpy as np
import jax
import jax.numpy as jnp
from jax import lax
from jax.experimental import pallas as pl
from jax.experimental.pallas import tpu as pltpu

F32 = jnp.float32
BF16 = jnp.bfloat16

D_MODEL = 1024
PAGE = 128
HD = 64
SLAB = 2 * HD
N_HEADS = 4
CMP_STRIDE = 16
CMP_LEN = 32
CMP_HIDDEN = 128
SEL_BLOCK = 64
SEL_TOP = 16
WINDOW = 512
Q_LORA = 256
KV_LORA = 128
NOPE = 64
ROPE = 32
ROPE_THETA = 10000.0
MOBA_BLOCK = 256
MOBA_TOP = 3
NUM_BUCKETS = 32
MAX_DISTANCE = 128
DEPTH = 2
D_FF = 2816
ALPHA = (2 * DEPTH) ** 0.25
N_IN_MIX = 2092
TILE = 256
NEG = -1e30
VMEM_LIMIT = 56 * 1024 * 1024

C_SBQ, C_SBKV, C_NSAQ, C_NSAKV, C_NSAG, C_QA, C_KVA, C_KR, C_KRS, C_MOBAQ, C_MOBAKV, C_END = (
    0, 512, 768, 1280, 1664, 1792, 2048, 2176, 2304, 2432, 2944, 3200)


def _cparams(sem, vmem=VMEM_LIMIT):
    return pltpu.CompilerParams(dimension_semantics=sem, vmem_limit_bytes=vmem)


def _dot(a, b):
    return jnp.dot(a, b, preferred_element_type=F32)


def _dot_nt(a, b):
    return lax.dot_general(a, b, (((1,), (1,)), ((), ())), preferred_element_type=F32)


def _split_dot(x, w):
    hi = x.astype(BF16)
    lo = (x - hi.astype(F32)).astype(BF16)
    return _dot(hi, w) + _dot(lo, w)


def _layer_norm(x, g, b):
    mu = jnp.mean(x, axis=-1, keepdims=True)
    xc = x - mu
    var = jnp.mean(xc * xc, axis=-1, keepdims=True)
    return xc * lax.rsqrt(var + 1e-5) * g + b


def _sigmoid(x):
    return 1.0 / (1.0 + jnp.exp(-x))


def _pad_heads(w):
    k = w.shape[1]
    return jnp.pad(w.reshape(N_HEADS, HD, k), ((0, 0), (0, HD), (0, 0))).reshape(N_HEADS * SLAB, k)


def _pad_rows(w, n):
    return jnp.pad(w, ((0, n - w.shape[0]), (0, 0)))


def _pack_layer(l, w_in, mla_w_uq, mla_w_uk, mla_w_uv, w_br, nsa_cmp_pe, nsa_cmp_w1, nsa_cmp_w2):
    wi = jnp.transpose(w_in, (2, 0, 1))[:, l, :]
    o = np.cumsum([0, 256, 256, 256, 384, 12, 256, 128, 32, 256, 256]).tolist()
    sb_q, sb_kv, nsa_q, nsa_kv, nsa_g, qa, kva, kr, moba_q, moba_kv = [wi[o[i]:o[i + 1]] for i in range(10)]
    kr_sw = jnp.concatenate([kr[ROPE // 2:], kr[:ROPE // 2]], axis=0)
    w_proj = jnp.concatenate([
        _pad_heads(sb_q), sb_kv, _pad_heads(nsa_q), nsa_kv, _pad_rows(nsa_g, 128), qa, kva,
        _pad_rows(kr, 128), _pad_rows(kr_sw, 128), _pad_heads(moba_q), moba_kv], axis=0).astype(BF16)
    w_gate = wi[N_IN_MIX:].astype(BF16)
    uq = mla_w_uq[l]
    uq_nope = uq[:, :, :NOPE].reshape(Q_LORA, N_HEADS * NOPE)
    uq_rope = uq[:, :, NOPE:]
    uq_rope_sw = jnp.concatenate([uq_rope[..., ROPE // 2:], uq_rope[..., :ROPE // 2]], axis=-1)
    place = lambda r: jnp.pad(r, ((0, 0), (0, 0), (0, 128 - ROPE))).reshape(Q_LORA, N_HEADS * 128)
    w_q3 = jnp.concatenate([uq_nope, place(uq_rope), place(uq_rope_sw)], axis=1).astype(BF16)
    uk = mla_w_uk[l]
    w_uk = jnp.zeros((N_HEADS * NOPE, N_HEADS * KV_LORA), F32)
    for h in range(N_HEADS):
        w_uk = w_uk.at[h * NOPE:(h + 1) * NOPE, h * KV_LORA:(h + 1) * KV_LORA].set(uk[:, h, :].T)
    w_uk = w_uk.astype(BF16)
    uv = mla_w_uv[l]
    w_uv = jnp.pad(jnp.transpose(uv, (1, 0, 2)), ((0, 0), (0, 0), (HD, 0))).astype(BF16)
    wb = w_br[l].reshape(4, N_HEADS, HD, D_MODEL)
    w_brp = jnp.pad(wb, ((0, 0), (0, 0), (HD, 0), (0, 0))).reshape(4, N_HEADS * SLAB, D_MODEL).astype(BF16)
    w1 = nsa_cmp_w1[l].reshape(2, 2, CMP_STRIDE, HD, CMP_HIDDEN)
    halves = []
    for half in range(2):
        wh = jnp.zeros((CMP_STRIDE, 4, HD, 2, CMP_HIDDEN), F32)
        for c in range(2):
            wh = wh.at[:, c, :, c, :].set(w1[c, half])
        halves.append(wh.reshape(CMP_STRIDE * 4 * HD, 2 * CMP_HIDDEN))
    w_c1 = jnp.concatenate(halves, axis=1).astype(BF16)
    w_c1d = jnp.concatenate(halves, axis=1).reshape(CMP_STRIDE, 4, HD, 4 * CMP_HIDDEN)[:, 0:2]
    w_c1d = w_c1d.reshape(CMP_STRIDE * 2 * HD, 4 * CMP_HIDDEN).astype(BF16)
    pe = nsa_cmp_pe[l].reshape(2, 2, CMP_STRIDE, HD)
    pe_rows = []
    for half in range(2):
        p = jnp.zeros((CMP_STRIDE, 4, HD), F32)
        for c in range(2):
            p = p.at[:, c, :].set(pe[c, half])
        pe_rows.append(p.reshape(1, CMP_STRIDE * 4 * HD))
    pe_rows = jnp.concatenate(pe_rows + [jnp.zeros((6, CMP_STRIDE * 4 * HD), F32)], axis=0)
    w2 = nsa_cmp_w2[l]
    w_c2 = jnp.zeros((2 * CMP_HIDDEN, 2 * HD), F32)
    for c in range(2):
        w_c2 = w_c2.at[c * CMP_HIDDEN:(c + 1) * CMP_HIDDEN, c * HD:(c + 1) * HD].set(w2[c])
    w_c2 = w_c2.astype(BF16)
    return dict(w_proj=w_proj, w_gate=w_gate, w_q3=w_q3, w_uk=w_uk, w_uv=w_uv, w_brp=w_brp,
                w_c1=w_c1, w_c1d=w_c1d, pe_rows=pe_rows, w_c2=w_c2)


def _rope_freq():
    half = ROPE // 2
    inv = ROPE_THETA ** (-np.arange(half, dtype=np.float32) / half)
    f = np.zeros((1, 128), np.float32)
    f[0, :half] = -inv
    f[0, half:ROPE] = inv
    return jnp.asarray(f)


def _ada_kernel(c_ref, w_ref, b_ref, o_ref):
    c = c_ref[...]
    act = (c * _sigmoid(c)).astype(BF16)
    o_ref[...] = _dot(act, w_ref[...].astype(BF16)) + b_ref[...]


def _ada(c, w, b):
    n = c.shape[0]
    npad = -(-n // 8) * 8
    cp = jnp.pad(c, ((0, npad - n), (0, 0)))
    cols = 6 * D_MODEL
    tn = 1536
    out = pl.pallas_call(
        _ada_kernel,
        grid=(cols // tn,),
        in_specs=[pl.BlockSpec((npad, D_MODEL), lambda j: (0, 0)),
                  pl.BlockSpec((D_MODEL, tn), lambda j: (0, j)),
                  pl.BlockSpec((1, tn), lambda j: (0, j))],
        out_specs=pl.BlockSpec((npad, tn), lambda j: (0, j)),
        out_shape=jax.ShapeDtypeStruct((npad, cols), F32),
        compiler_params=_cparams(("arbitrary",)),
        name="ada",
    )(cp, w, b.reshape(1, cols))
    return out[:n]


def _proj_kernel(x_ref, mod_ref, lng_ref, lnb_ref, w_ref, qn_ref, kvn_ref, wq3_ref, wuk_ref, freq_ref, *outs,
                 apply_ln, pos0, period, emit_kbar):
    outs = list(outs)
    (sbq_o, sbkv_o, sbkvb_o, nsaq_o, nsakv_o, nsawin_o, nsakvb_o, nsag_o, mlar_o, mlarb_o, mlaq_o,
     mobaq_o, mobakv_o, mobakvb_o) = outs[:14]
    rest = outs[14:]
    tm = x_ref.shape[1]
    x = x_ref[0]
    if apply_ln:
        x = _layer_norm(x, lng_ref[...], lnb_ref[...])
        rest.pop(0)[0] = x
    mod = mod_ref[0]
    u = (x * (1.0 + mod[:, D_MODEL:2 * D_MODEL]) + mod[:, 0:D_MODEL]).astype(BF16)

    def grp(a, b):
        return _dot_nt(u, w_ref[a:b, :])

    qscale = HD ** -0.5
    sbq_o[0] = (grp(C_SBQ, C_SBKV) * qscale).astype(BF16)
    kv = grp(C_SBKV, C_NSAQ)
    sbkv_o[0] = kv
    sbkvb_o[0] = kv.astype(BF16)
    nsaq_o[0] = (grp(C_NSAQ, C_NSAKV) * qscale).astype(BF16)
    kv = grp(C_NSAKV, C_NSAG)
    nsakv_o[0] = kv[:, 0:256]
    nsawin_o[0] = kv[:, 256:384]
    nsakvb_o[0] = kv.astype(BF16)
    nsag_o[0] = _sigmoid(grp(C_NSAG, C_QA))
    mobaq_o[0] = (grp(C_MOBAQ, C_MOBAKV) * qscale).astype(BF16)
    kv = grp(C_MOBAKV, C_END)
    mobakv_o[0] = kv
    mobakvb_o[0] = kv.astype(BF16)
    if emit_kbar:
        rest.pop(0)[0, 0] = jnp.mean(kv, axis=0, keepdims=True)

    row = lax.broadcasted_iota(jnp.int32, (tm, 1), 0) + pl.program_id(1) * tm
    pos = (pos0 + (row & (period - 1))).astype(F32)
    ang = pos * freq_ref[...]
    cs, sn = jnp.cos(ang), jnp.sin(ang)

    kva = grp(C_KVA, C_KR)
    ckv = kva * lax.rsqrt(jnp.mean(kva * kva, axis=-1, keepdims=True) + 1e-6) * kvn_ref[...]
    kr = grp(C_KR, C_KRS) * cs + grp(C_KRS, C_MOBAQ) * sn
    mlar_o[0, :, 0:KV_LORA] = ckv
    mlar_o[0, :, KV_LORA:KV_LORA + ROPE] = kr[:, 0:ROPE]
    last = lax.broadcasted_iota(jnp.int32, kr.shape, 1) == 127
    mlarb_o[0] = jnp.concatenate([ckv, jnp.where(last, 1.0, kr)], axis=1).astype(BF16)

    qa = grp(C_QA, C_KVA)
    qan = (qa * lax.rsqrt(jnp.mean(qa * qa, axis=-1, keepdims=True) + 1e-6) * qn_ref[...]).astype(BF16)
    q3 = _dot(qan, wq3_ref[...])
    qlat = _dot(q3[:, 0:256].astype(BF16), wuk_ref[...])
    mscale = (NOPE + ROPE) ** -0.5
    for h in range(N_HEADS):
        a = q3[:, 256 + 128 * h:384 + 128 * h]
        b = q3[:, 768 + 128 * h:896 + 128 * h]
        mlaq_o[0, :, 256 * h:256 * h + 128] = (qlat[:, 128 * h:128 * h + 128] * mscale).astype(BF16)
        mlaq_o[0, :, 256 * h + 128:256 * h + 256] = ((a * cs + b * sn) * mscale).astype(BF16)


def _proj(x, mod, lng, lnb, pk, qn, kvn, *, apply_ln, pos0, period, mod_per_row, emit_kbar):
    nb, r, _ = x.shape
    tm = min(TILE, r)
    grid = (nb, r // tm)
    row = lambda c, dt: (pl.BlockSpec((1, tm, c), lambda n, i: (n, i, 0)), jax.ShapeDtypeStruct((nb, r, c), dt))
    outs = [row(512, BF16), row(256, F32), row(256, BF16), row(512, BF16), row(256, F32), row(128, F32),
            row(384, BF16), row(128, F32), row(KV_LORA + ROPE, F32), row(256, BF16), row(1024, BF16),
            row(512, BF16), row(256, F32), row(256, BF16)]
    if apply_ln:
        outs.append(row(D_MODEL, F32))
    if emit_kbar:
        outs.append((pl.BlockSpec((1, 1, 1, 256), lambda n, i: (n, i, 0, 0)),
                     jax.ShapeDtypeStruct((nb, r // tm, 1, 256), F32)))
    full = lambda a: pl.BlockSpec(a.shape, lambda n, i: (0,) * a.ndim)
    if mod_per_row:
        mod_spec = pl.BlockSpec((1, tm, 6 * D_MODEL), lambda n, i: (n, i, 0))
    else:
        mod_spec = pl.BlockSpec((1, 1, 6 * D_MODEL), lambda n, i: (n, 0, 0))
    freq = _rope_freq()
    args = [x, mod, lng.reshape(1, -1), lnb.reshape(1, -1), pk["w_proj"], qn.reshape(1, -1), kvn.reshape(1, -1),
            pk["w_q3"], pk["w_uk"], freq]
    in_specs = [pl.BlockSpec((1, tm, D_MODEL), lambda n, i: (n, i, 0)), mod_spec] + [full(a) for a in args[2:]]
    res = pl.pallas_call(
        functools.partial(_proj_kernel, apply_ln=apply_ln, pos0=pos0, period=period, emit_kbar=emit_kbar),
        grid=grid, in_specs=in_specs,
        out_specs=[o[0] for o in outs], out_shape=[o[1] for o in outs],
        compiler_params=_cparams(("parallel", "parallel")),
        name="proj",
    )(*args)
    return res


def _merge_kernel(x_ref, mod_ref, o0, o1, o2, o3, wg_ref, wbr_ref, wo_ref, g_ref, b_ref, out_ref):
    x = x_ref[0]
    mod = mod_ref[0]
    u = (x * (1.0 + mod[:, D_MODEL:2 * D_MODEL]) + mod[:, 0:D_MODEL]).astype(BF16)
    merged = None
    for bi, o in enumerate((o0, o1, o2, o3)):
        gate = _sigmoid(_dot_nt(u, wg_ref[bi * D_MODEL:(bi + 1) * D_MODEL, :]))
        y = gate * _dot(o[0].astype(BF16), wbr_ref[bi])
        merged = y if merged is None else merged + y
    mix = _dot(merged.astype(BF16), wo_ref[...])
    out_ref[0] = _layer_norm(ALPHA * x + mod[:, 2 * D_MODEL:3 * D_MODEL] * mix, g_ref[...], b_ref[...])


def _ffn_kernel(x_ref, mod_ref, w1_ref, w3_ref, w2_ref, g_ref, b_ref, out_ref):
    x = x_ref[0]
    mod = mod_ref[0]
    u = (x * (1.0 + mod[:, 4 * D_MODEL:5 * D_MODEL]) + mod[:, 3 * D_MODEL:4 * D_MODEL]).astype(BF16)
    a = _dot(u, w1_ref[...])
    h = (a * _sigmoid(a) * _dot(u, w3_ref[...])).astype(BF16)
    f = _dot(h, w2_ref[...])
    out_ref[0] = _layer_norm(ALPHA * x + mod[:, 5 * D_MODEL:6 * D_MODEL] * f, g_ref[...], b_ref[...])


def _row_call(kernel, name, x, mod, mod_per_row, row_ins, consts):
    nb, r, _ = x.shape
    tm = min(TILE, r)
    if mod_per_row:
        mod_spec = pl.BlockSpec((1, tm, 6 * D_MODEL), lambda n, i: (n, i, 0))
    else:
        mod_spec = pl.BlockSpec((1, 1, 6 * D_MODEL), lambda n, i: (n, 0, 0))
    rspec = lambda a: pl.BlockSpec((1, tm, a.shape[-1]), lambda n, i: (n, i, 0))
    full = lambda a: pl.BlockSpec(a.shape, lambda n, i: (0,) * a.ndim, pipeline_mode=pl.Buffered(1))
    return pl.pallas_call(
        kernel, grid=(nb, r // tm),
        in_specs=[rspec(x), mod_spec] + [rspec(a) for a in row_ins] + [full(a) for a in consts],
        out_specs=rspec(x), out_shape=jax.ShapeDtypeStruct(x.shape, F32),
        compiler_params=_cparams(("parallel", "parallel")),
        name=name,
    )(x, mod, *row_ins, *consts)


def _neg_softplus(z):
    return -(jnp.maximum(z, 0.0) + jnp.log(1.0 + jnp.exp(-jnp.abs(z))))


def _sb_tile(z, kvt, tri, carry, mask):
    lr = _neg_softplus(z)
    if mask is not None:
        lr = jnp.where(mask, lr, 0.0)
    tail = _split_dot(lr, tri)
    w = jnp.exp(z + lr + tail + carry)
    if mask is not None:
        w = jnp.where(mask, w, 0.0)
    contrib = _dot(w.astype(BF16), kvt)
    return contrib, carry + tail[:, 0:1] + lr[:, 0:1]


def _rep2(x):
    return jnp.concatenate([x, x], axis=1)


def _sb_tile_rep(z, kvt, trij, carry, mask):
    tk = z.shape[1]
    lr = _neg_softplus(z)
    if mask is not None:
        lr = jnp.where(mask, lr, 0.0)
    tl = _split_dot(lr, trij)
    w = jnp.exp(z + lr + tl[:, 0:tk] + _rep2(carry))
    if mask is not None:
        w = jnp.where(mask, w, 0.0)
    return _dot(w.astype(BF16), kvt), carry + tl[:, tk:tk + 128]


def _sb_prompt_kernel(q_ref, kv_ref, tri_ref, o_ref, acc_ref, carry_ref):
    tq = q_ref.shape[1]
    i = pl.program_id(1)
    rowq = lax.broadcasted_iota(jnp.int32, (2 * tq, tq), 0) & (tq - 1)
    col = lax.broadcasted_iota(jnp.int32, (2 * tq, tq), 1)
    q2 = [jnp.concatenate([q_ref[0, :, SLAB * (2 * kvh):SLAB * (2 * kvh + 1)],
                           q_ref[0, :, SLAB * (2 * kvh + 1):SLAB * (2 * kvh + 2)]], axis=0) for kvh in range(2)]

    def kv_tile(j, kvh):
        return kv_ref[0, pl.ds(pl.multiple_of(j * tq, tq), tq), SLAB * kvh:SLAB * (kvh + 1)]

    for kvh in range(2):
        kvt = kv_tile(i, kvh)
        contrib, carry = _sb_tile_rep(_dot_nt(q2[kvh], kvt), kvt, tri_ref[...], jnp.zeros((2 * tq, 128), F32),
                                      col < rowq)
        acc_ref[kvh] = contrib
        carry_ref[kvh] = carry

    def past_tile(j):
        for kvh in range(2):
            kvt = kv_tile(j, kvh)
            contrib, carry = _sb_tile_rep(_dot_nt(q2[kvh], kvt), kvt, tri_ref[...], carry_ref[kvh], None)
            acc_ref[kvh] += contrib
            carry_ref[kvh] = carry

    _pair_loop(i, past_tile)
    for kvh in range(2):
        acc = acc_ref[kvh].astype(BF16)
        o_ref[0, :, SLAB * (2 * kvh):SLAB * (2 * kvh + 1)] = acc[0:tq]
        o_ref[0, :, SLAB * (2 * kvh + 1):SLAB * (2 * kvh + 2)] = acc[tq:2 * tq]


def _tri(n, with_total=False):
    r = np.arange(n)
    t = (r[:, None] > r[None, :]).astype(np.float32)
    if with_total:
        t = np.concatenate([t, np.ones((n, 128), np.float32)], axis=1)
    return jnp.asarray(t, dtype=BF16)


def _sb_prompt(q, kvb):
    nb, t, _ = q.shape
    tq = TILE
    return pl.pallas_call(
        _sb_prompt_kernel, grid=(nb, t // tq),
        in_specs=[pl.BlockSpec((1, tq, 512), lambda n, i: (n, i, 0)),
                  pl.BlockSpec((1, t, 256), lambda n, i: (n, 0, 0)),
                  pl.BlockSpec((tq, tq + 128), lambda n, i: (0, 0))],
        out_specs=pl.BlockSpec((1, tq, 512), lambda n, i: (n, i, 0)),
        out_shape=jax.ShapeDtypeStruct((nb, t, 512), BF16),
        scratch_shapes=[pltpu.VMEM((2, 2 * tq, SLAB), F32), pltpu.VMEM((2, 2 * tq, 128), F32)],
        compiler_params=_cparams(("parallel", "arbitrary")),
        name="sb_prompt",
    )(q, kvb, _tri(tq, with_total=True))


def _softmax_step(s, vt, m_ref, l_ref, acc_ref, first, v_is_transposed=False):
    pv = _dot_nt if v_is_transposed else _dot
    if first:
        m_new = jnp.max(s, axis=1, keepdims=True)
        p = jnp.exp(s - m_new)
        l_ref[...] = jnp.sum(p, axis=1, keepdims=True)
        acc_ref[...] = pv(p.astype(BF16), vt)
    else:
        m_old = m_ref[...]
        m_new = jnp.maximum(m_old, jnp.max(s, axis=1, keepdims=True))
        a = jnp.exp(m_old - m_new)
        p = jnp.exp(s - m_new)
        l_ref[...] = a * l_ref[...] + jnp.sum(p, axis=1, keepdims=True)
        acc_ref[...] = a * acc_ref[...] + pv(p.astype(BF16), vt)
    m_ref[...] = m_new


def _pair_loop(n, tile_fn):
    @pl.when(n % 2 == 1)
    def _():
        tile_fn(n - 1)

    def body(k, _):
        j = n - (n % 2) - 1 - 2 * k
        tile_fn(j)
        tile_fn(j - 1)
        return 0

    lax.fori_loop(0, n // 2, body, 0)


def _lane_max(s):
    return jnp.maximum(s[:, 0:128], s[:, 128:256])


def _row_max_rep(mrun):
    return jnp.broadcast_to(jnp.max(mrun, axis=1, keepdims=True), mrun.shape)


def _mla_prompt_kernel(q_ref, rows_ref, wuv_ref, o_ref, m_ref, acc_ref):
    tq = q_ref.shape[1]
    i = pl.program_id(1)
    q4 = jnp.concatenate([q_ref[0, :, 256 * h:256 * (h + 1)] for h in range(N_HEADS)], axis=0)
    rowq = lax.broadcasted_iota(jnp.int32, (N_HEADS * tq, tq), 0) & (tq - 1)
    col = lax.broadcasted_iota(jnp.int32, (N_HEADS * tq, tq), 1)

    def rows_tile(j):
        return rows_ref[0, pl.ds(pl.multiple_of(j * tq, tq), tq), :]

    rt = rows_tile(i)
    s_diag = jnp.where(col <= rowq, _dot_nt(q4, rt), NEG)
    m_ref[...] = _lane_max(s_diag)

    def max_tile(j):
        m_ref[...] = jnp.maximum(m_ref[...], _lane_max(_dot_nt(q4, rows_tile(j))))

    def acc_tile(j):
        rt = rows_tile(j)
        acc_ref[...] += _dot(jnp.exp(_dot_nt(q4, rt) - _rep2(m_ref[...])).astype(BF16), rt)

    _pair_loop(i, max_tile)
    m_ref[...] = _row_max_rep(m_ref[...])
    acc_ref[...] = _dot(jnp.exp(s_diag - _rep2(m_ref[...])).astype(BF16), rt)
    _pair_loop(i, acc_tile)
    olat = (acc_ref[:, 0:KV_LORA] / jnp.maximum(acc_ref[:, 255:256], 1e-30)).astype(BF16)
    for h in range(N_HEADS):
        o_ref[0, :, SLAB * h:SLAB * (h + 1)] = _dot(olat[h * tq:(h + 1) * tq], wuv_ref[h]).astype(BF16)


def _mla_prompt(q, rowsb, w_uv):
    nb, t, _ = q.shape
    tq = TILE
    return pl.pallas_call(
        _mla_prompt_kernel, grid=(nb, t // tq),
        in_specs=[pl.BlockSpec((1, tq, 1024), lambda n, i: (n, i, 0)),
                  pl.BlockSpec((1, t, 256), lambda n, i: (n, 0, 0)),
                  pl.BlockSpec((N_HEADS, 128, 128), lambda n, i: (0, 0, 0))],
        out_specs=pl.BlockSpec((1, tq, 512), lambda n, i: (n, i, 0)),
        out_shape=jax.ShapeDtypeStruct((nb, t, 512), BF16),
        scratch_shapes=[pltpu.VMEM((N_HEADS * tq, 128), F32), pltpu.VMEM((N_HEADS * tq, 256), F32)],
        compiler_params=_cparams(("parallel", "arbitrary")),
        name="mla_prompt",
    )(q, rowsb, w_uv)


def _t5_bucket(n):
    exact = NUM_BUCKETS // 2
    nf = jnp.maximum(n, 1).astype(F32)
    far = exact + (jnp.log(nf / exact) / math.log(MAX_DISTANCE / exact) * (NUM_BUCKETS - exact)).astype(jnp.int32)
    return jnp.where(n < exact, n, jnp.minimum(far, NUM_BUCKETS - 1))


def _bias_kernel(tab_ref, lut_ref, tile_ref):
    h = pl.program_id(0)
    t = tile_ref.shape[2]

    def lookup(n):
        bucket = _t5_bucket(jnp.maximum(n, 0))
        out = jnp.zeros(n.shape, F32)
        for b in range(NUM_BUCKETS):
            out = jnp.where(bucket == b, tab_ref[b, h], out)
        return out

    lut_ref[0] = lookup(lax.broadcasted_iota(jnp.int32, (8, 128), 1))
    rel = lax.broadcasted_iota(jnp.int32, (t, t), 0) - lax.broadcasted_iota(jnp.int32, (t, t), 1)
    tile_ref[0, 0] = lookup(rel)
    tile_ref[0, 1] = lookup(rel + t)


def _bias_tables(rel_bias):
    nh = rel_bias.shape[1]
    lut, tiles = pl.pallas_call(
        _bias_kernel, grid=(nh,),
        in_specs=[pl.BlockSpec(memory_space=pltpu.SMEM)],
        out_specs=[pl.BlockSpec((1, 8, 128), lambda h: (h, 0, 0)),
                   pl.BlockSpec((1, 2, TILE, TILE), lambda h: (h, 0, 0, 0))],
        out_shape=[jax.ShapeDtypeStruct((nh, 8, 128), F32), jax.ShapeDtypeStruct((nh, 2, TILE, TILE), F32)],
        compiler_params=_cparams(("arbitrary",)),
        name="bias_tables",
    )(rel_bias)
    return lut[:, 0, :], tiles


def _group_tiles(tiles, far, groups):
    h = tiles.shape[0]
    g = h // groups
    t = tiles.shape[2]
    bt = tiles.reshape(groups, g, 2, t, t).transpose(0, 2, 1, 3, 4).reshape(groups, 2, g * t, t)
    fr = jnp.broadcast_to(far.reshape(groups, g, 1, 1), (groups, g, t, 1)).reshape(groups, g * t, 1)
    return bt, fr


def _topk_ids(score, k):
    col = lax.broadcasted_iota(jnp.int32, score.shape, 1)
    ids = []
    for _ in range(k):
        m = jnp.max(score, axis=1, keepdims=True)
        first = jnp.min(jnp.where(score == m, col, 1 << 20), axis=1, keepdims=True)
        ids.append(jnp.where(m > 0.5 * NEG, first, -1))
        score = jnp.where(col == first, NEG, score)
    return ids


def _topk_ids_t(score_t, k):
    row = lax.broadcasted_iota(jnp.int32, score_t.shape, 0)
    ids = []
    for _ in range(k):
        m = jnp.max(score_t, axis=0, keepdims=True)
        first = jnp.min(jnp.where(score_t == m, row, 1 << 20), axis=0, keepdims=True)
        ids.append(jnp.where(m > 0.5 * NEG, first, -1))
        score_t = jnp.where(row == first, NEG, score_t)
    return ids


def _with_ones(kvt):
    return jnp.concatenate([kvt, jnp.ones(kvt.shape, kvt.dtype)], axis=1)


def _where_rep(chosen, s, other):
    return jnp.concatenate([jnp.where(chosen, s[:, 0:128], other), jnp.where(chosen, s[:, 128:256], other)], axis=1)


def _moba_prompt_kernel(q_ref, kv_ref, kbar_ref, bt_ref, far_ref, o_ref, m_ref, acc_ref, id_ref):
    tq = q_ref.shape[1]
    i = pl.program_id(1)
    rowq = lax.broadcasted_iota(jnp.int32, (2 * tq, tq), 0) & (tq - 1)
    col = lax.broadcasted_iota(jnp.int32, (2 * tq, tq), 1)
    brow = lax.broadcasted_iota(jnp.int32, (kbar_ref.shape[1], 2 * tq), 0)
    q2 = [jnp.concatenate([q_ref[0, :, SLAB * (2 * kvh):SLAB * (2 * kvh + 1)],
                           q_ref[0, :, SLAB * (2 * kvh + 1):SLAB * (2 * kvh + 2)]], axis=0) for kvh in range(2)]
    for kvh in range(2):
        kb = kbar_ref[0, :, SLAB * kvh:SLAB * (kvh + 1)]
        kb_hi = kb.astype(BF16)
        kb_lo = (kb - kb_hi.astype(F32)).astype(BF16)
        gs_t = _dot_nt(kb_hi, q2[kvh]) + _dot_nt(kb_lo, q2[kvh])
        ids = _topk_ids_t(jnp.where(brow < i, gs_t, NEG), MOBA_TOP)
        for r in range(MOBA_TOP):
            id_ref[kvh, r] = jnp.broadcast_to(ids[r].astype(F32), (128, 2 * tq)).T
    far2 = [jnp.broadcast_to(far_ref[kvh], (2 * tq, tq)) for kvh in range(2)]

    def kv_tile(j, kvh):
        return kv_ref[0, pl.ds(pl.multiple_of(j * tq, tq), tq), SLAB * kvh:SLAB * (kvh + 1)]

    def past_scores(j, kvh):
        kvt = kv_tile(j, kvh)
        jf = j.astype(F32)
        chosen = (id_ref[kvh, 0] == jf) | (id_ref[kvh, 1] == jf) | (id_ref[kvh, 2] == jf)
        bias = jnp.where(j == i - 1, bt_ref[kvh, 1], far2[kvh])
        return _where_rep(chosen, _dot_nt(q2[kvh], kvt) + bias, NEG), kvt

    kv_diag = [kv_tile(i, kvh) for kvh in range(2)]
    s_diag = [jnp.where(col <= rowq, _dot_nt(q2[kvh], kv_diag[kvh]) + bt_ref[kvh, 0], NEG) for kvh in range(2)]
    for kvh in range(2):
        m_ref[kvh] = _lane_max(s_diag[kvh])

    def max_tile(j):
        for kvh in range(2):
            m_ref[kvh] = jnp.maximum(m_ref[kvh], _lane_max(past_scores(j, kvh)[0]))

    _pair_loop(i, max_tile)
    for kvh in range(2):
        m_ref[kvh] = _row_max_rep(m_ref[kvh])
        acc_ref[kvh] = _dot(jnp.exp(s_diag[kvh] - _rep2(m_ref[kvh])).astype(BF16), _with_ones(kv_diag[kvh]))

    def acc_tile(j):
        for kvh in range(2):
            s, kvt = past_scores(j, kvh)
            acc_ref[kvh] += _dot(jnp.exp(s - _rep2(m_ref[kvh])).astype(BF16), _with_ones(kvt))

    _pair_loop(i, acc_tile)
    for kvh in range(2):
        out = (acc_ref[kvh, :, 0:SLAB] / jnp.maximum(acc_ref[kvh, :, SLAB:2 * SLAB], 1e-30)).astype(BF16)
        o_ref[0, :, SLAB * (2 * kvh):SLAB * (2 * kvh + 1)] = out[0:tq]
        o_ref[0, :, SLAB * (2 * kvh + 1):SLAB * (2 * kvh + 2)] = out[tq:2 * tq]


def _moba_prompt(q, kvb, kbar, bt, far):
    nb, t, _ = q.shape
    tq = TILE
    assert tq == MOBA_BLOCK
    full = lambda a: pl.BlockSpec(a.shape, lambda n, i: (0,) * a.ndim)
    return pl.pallas_call(
        _moba_prompt_kernel, grid=(nb, t // tq),
        in_specs=[pl.BlockSpec((1, tq, 512), lambda n, i: (n, i, 0)),
                  pl.BlockSpec((1, t, 256), lambda n, i: (n, 0, 0)),
                  pl.BlockSpec((1,) + kbar.shape[1:], lambda n, i: (n, 0, 0)),
                  full(bt), full(far)],
        out_specs=pl.BlockSpec((1, tq, 512), lambda n, i: (n, i, 0)),
        out_shape=jax.ShapeDtypeStruct((nb, t, 512), BF16),
        scratch_shapes=[pltpu.VMEM((2, 2 * tq, 128), F32), pltpu.VMEM((2, 2 * tq, 2 * SLAB), F32),
                        pltpu.VMEM((2, MOBA_TOP, 2 * tq, 128), F32)],
        compiler_params=_cparams(("parallel", "arbitrary")),
        name="moba_prompt",
    )(q, kvb, kbar, bt, far)


def _cmp_a_kernel(x_ref, w_ref, o_ref):
    o_ref[0] = _dot(x_ref[0].astype(BF16), w_ref[...])


def _cmp_a(x, w_c1):
    nb, j1, c = x.shape
    tj = min(TILE, j1)
    return pl.pallas_call(
        _cmp_a_kernel, grid=(nb, j1 // tj),
        in_specs=[pl.BlockSpec((1, tj, c), lambda n, i: (n, i, 0)),
                  pl.BlockSpec(w_c1.shape, lambda n, i: (0, 0), pipeline_mode=pl.Buffered(1))],
        out_specs=pl.BlockSpec((1, tj, 512), lambda n, i: (n, i, 0)),
        out_shape=jax.ShapeDtypeStruct((nb, j1, 512), F32),
        compiler_params=_cparams(("parallel", "parallel")),
        name="cmp_a",
    )(x, w_c1)


def _gelu_tanh(x):
    return 0.5 * x * (1.0 + jnp.tanh(math.sqrt(2.0 / math.pi) * (x + 0.044715 * (x * x * x))))


def _cmp_b_kernel(fs_ref, pe_ref, w1_ref, w2_ref, o_ref):
    j1 = fs_ref.shape[1]
    pe = _dot(pe_ref[...].astype(BF16), w1_ref[...])
    pe_term = pe[0:1, 0:256] + pe[1:2, 256:512]
    first = fs_ref[0, :, 0:256]
    nxt = pltpu.roll(fs_ref[0, :, 256:512], j1 - 1, 0)
    h = _gelu_tanh(first + nxt + pe_term)
    o_ref[0] = _dot(h.astype(BF16), w2_ref[...]).astype(BF16)


def _cmp_b(fs, pe_rows, w_c1, w_c2):
    nb, j1, _ = fs.shape
    full = lambda a: pl.BlockSpec(a.shape, lambda n: (0,) * a.ndim)
    return pl.pallas_call(
        _cmp_b_kernel, grid=(nb,),
        in_specs=[pl.BlockSpec((1, j1, 512), lambda n: (n, 0, 0)), full(pe_rows), full(w_c1), full(w_c2)],
        out_specs=pl.BlockSpec((1, j1, SLAB), lambda n: (n, 0, 0)),
        out_shape=jax.ShapeDtypeStruct((nb, j1, SLAB), BF16),
        compiler_params=_cparams(("parallel",)),
        name="cmp_b",
    )(fs, pe_rows, w_c1, w_c2)


def _lut_gather(lut_row, idx):
    m, c = idx.shape
    src = jnp.broadcast_to(lut_row, (m, 128))
    return jnp.concatenate([jnp.take_along_axis(src, idx[:, 128 * k:128 * (k + 1)], axis=1)
                            for k in range(c // 128)], axis=1)


def _masked_softmax(s, valid):
    m = jnp.max(s, axis=1, keepdims=True)
    e = jnp.where(valid, jnp.exp(s - m), 0.0)
    return e / jnp.maximum(jnp.sum(e, axis=1, keepdims=True), 1e-30)


def _select_blocks_t(imp_t, qpos):
    b = lax.broadcasted_iota(jnp.int32, imp_t.shape, 0)
    cur = qpos >> 6
    forced = (b == 0) | (b == cur) | (b == cur - 1)
    score = jnp.where(forced, -NEG, jnp.where(b <= cur, imp_t, NEG))
    sel = jnp.zeros(imp_t.shape, F32)
    for _ in range(SEL_TOP):
        m = jnp.max(score, axis=0, keepdims=True)
        first = jnp.min(jnp.where(score == m, b, 1 << 20), axis=0, keepdims=True)
        hit = (b == first) & (m > 0.5 * NEG)
        sel = jnp.where(hit, 1.0, sel)
        score = jnp.where(hit, NEG, score)
    return sel


def _nsa_prompt_kernel(q_ref, kv_ref, cmp_ref, g_ref, lut_ref, cover_ref, e_ref, bt_ref, far_ref, o_ref,
                       oc_ref, os_ref, mx_ref, acc2_ref):
    tq = q_ref.shape[1]
    j1 = cmp_ref.shape[1]
    i = pl.program_id(1)
    nh = N_HEADS
    q4 = jnp.concatenate([q_ref[0, :, SLAB * h:SLAB * (h + 1)] for h in range(nh)], axis=0)
    rowq = lax.broadcasted_iota(jnp.int32, (tq, tq), 0)
    col = lax.broadcasted_iota(jnp.int32, (tq, tq), 1)
    rep = lambda x: jnp.concatenate([x] * nh, axis=0)
    qpos = i * tq + lax.broadcasted_iota(jnp.int32, (tq, 1), 0)

    cmp = cmp_ref[0]
    jcol = lax.broadcasted_iota(jnp.int32, (tq, j1), 1)
    rel_c = qpos - (jcol * CMP_STRIDE + (CMP_LEN - 1))
    valid = (rel_c >= 0) & (jcol < j1 - 1)
    idx = jnp.clip(rel_c, 0, 127)
    bias = jnp.concatenate([_lut_gather(lut_ref[h:h + 1, :], idx) for h in range(nh)], axis=0)
    valid4 = rep(valid)
    pc = _masked_softmax(jnp.where(valid4, _dot_nt(q4, cmp) + bias, NEG), valid4)
    oc_ref[...] = _dot(pc.astype(BF16), cmp)
    psum = pc[0:tq] + pc[tq:2 * tq] + pc[2 * tq:3 * tq] + pc[3 * tq:4 * tq]
    p_hi = psum.astype(BF16)
    p_lo = (psum - p_hi.astype(F32)).astype(BF16)
    imp_t = _dot_nt(cover_ref[...], p_hi) + _dot_nt(cover_ref[...], p_lo)
    qpos_row = i * tq + lax.broadcasted_iota(jnp.int32, (1, tq), 1)
    selb = _select_blocks_t(imp_t, qpos_row).T.astype(BF16)

    def kv_tile(j, part):
        return kv_ref[0, pl.ds(pl.multiple_of(j * tq, tq), tq), SLAB * part:SLAB * (part + 1)]

    far2 = jnp.broadcast_to(far_ref[...], (nh * tq, tq))

    def past_scores(j):
        kvt = kv_tile(j, 1)
        bias = jnp.where(j == i - 1, bt_ref[1], far2)
        ok = _dot(selb, e_ref[j]) > 0.5
        return jnp.where(rep(ok), _dot_nt(q4, kvt) + bias, NEG), kvt

    kvt = kv_tile(i, 1)
    ok = (_dot(selb, e_ref[i]) > 0.5) & (col <= rowq)
    s_diag = jnp.where(rep(ok), _dot_nt(q4, kvt) + bt_ref[0], NEG)
    mx_ref[...] = _lane_max(s_diag)

    def max_tile(j):
        mx_ref[...] = jnp.maximum(mx_ref[...], _lane_max(past_scores(j)[0]))

    def acc_tile(j):
        s, kvt = past_scores(j)
        acc2_ref[...] += _dot(jnp.exp(s - _rep2(mx_ref[...])).astype(BF16), _with_ones(kvt))

    _pair_loop(i, max_tile)
    mx_ref[...] = _row_max_rep(mx_ref[...])
    acc2_ref[...] = _dot(jnp.exp(s_diag - _rep2(mx_ref[...])).astype(BF16), _with_ones(kvt))
    _pair_loop(i, acc_tile)
    os_ref[...] = acc2_ref[:, 0:SLAB] / jnp.maximum(acc2_ref[:, SLAB:2 * SLAB], 1e-30)

    kv0, kv1, kv2 = kv_tile(i, 2), kv_tile(jnp.maximum(i - 1, 0), 2), kv_tile(jnp.maximum(i - 2, 0), 2)
    s0 = jnp.where(rep(col <= rowq), _dot_nt(q4, kv0) + bt_ref[0], NEG)
    s1 = jnp.where(i >= 1, _dot_nt(q4, kv1) + bt_ref[1], NEG)
    s2 = jnp.where(rep(col > rowq) & (i >= 2), _dot_nt(q4, kv2) + far2, NEG)
    mw = _row_max_rep(jnp.maximum(jnp.maximum(_lane_max(s0), _lane_max(s1)), _lane_max(s2)))
    mw2 = _rep2(mw)
    accw = (_dot(jnp.exp(s0 - mw2).astype(BF16), _with_ones(kv0))
            + _dot(jnp.exp(s1 - mw2).astype(BF16), _with_ones(kv1))
            + _dot(jnp.exp(s2 - mw2).astype(BF16), _with_ones(kv2)))
    ow = accw[:, 0:SLAB] / jnp.maximum(accw[:, SLAB:2 * SLAB], 1e-30)
    g = g_ref[0]
    for h in range(nh):
        r = slice(h * tq, (h + 1) * tq)
        o = (g[:, h:h + 1] * oc_ref[r, :] + g[:, nh + h:nh + h + 1] * os_ref[r, :]
             + g[:, 2 * nh + h:2 * nh + h + 1] * ow[r])
        o_ref[0, :, SLAB * h:SLAB * (h + 1)] = o.astype(BF16)


def _cover(j1, nb_pad):
    j = np.arange(j1)[:, None]
    b = np.arange(nb_pad)[None, :]
    cstart, cend, bstart = j * CMP_STRIDE, j * CMP_STRIDE + CMP_LEN - 1, b * SEL_BLOCK
    return jnp.asarray(((cstart < bstart + SEL_BLOCK) & (cend >= bstart)).astype(np.float32), dtype=BF16)


def _expand(n_tiles, tk, nb_pad):
    key = np.arange(n_tiles * tk).reshape(n_tiles, 1, tk)
    b = np.arange(nb_pad).reshape(1, nb_pad, 1)
    return jnp.asarray((key // SEL_BLOCK == b).astype(np.float32), dtype=BF16)


def _nsa_prompt(q, kvb, cmp, gates, lut, bt, far):
    nb, t, _ = q.shape
    tq = TILE
    assert WINDOW == 2 * tq and t // SEL_BLOCK <= 128 and t // SEL_BLOCK >= SEL_TOP
    j1 = cmp.shape[1]
    cover = _cover(j1, 128).T
    e = _expand(t // tq, tq, 128)
    full = lambda a: pl.BlockSpec(a.shape, lambda n, i: (0,) * a.ndim)
    return pl.pallas_call(
        _nsa_prompt_kernel, grid=(nb, t // tq),
        in_specs=[pl.BlockSpec((1, tq, 512), lambda n, i: (n, i, 0)),
                  pl.BlockSpec((1, t, 384), lambda n, i: (n, 0, 0)),
                  pl.BlockSpec((1, j1, SLAB), lambda n, i: (n, 0, 0)),
                  pl.BlockSpec((1, tq, 128), lambda n, i: (n, i, 0)),
                  full(lut), full(cover), full(e), full(bt), full(far)],
        out_specs=pl.BlockSpec((1, tq, 512), lambda n, i: (n, i, 0)),
        out_shape=jax.ShapeDtypeStruct((nb, t, 512), BF16),
        scratch_shapes=[pltpu.VMEM((N_HEADS * tq, SLAB), F32), pltpu.VMEM((N_HEADS * tq, SLAB), F32),
                        pltpu.VMEM((N_HEADS * tq, 128), F32), pltpu.VMEM((N_HEADS * tq, 2 * SLAB), F32)],
        compiler_params=_cparams(("parallel", "arbitrary")),
        name="nsa_prompt",
    )(q, kvb, cmp, gates, lut, cover, e, bt, far)


def kernel(x_prompt, x_sample, cache_sb_kv, cache_nsa_kv, state_nsa_win, cache_mla, cache_moba_kv, page_table, c_prompt, c_sample, rel_bias, ln_in_g, ln_in_b, w_ada, b_ada, w_in, nsa_cmp_pe, nsa_cmp_w1, nsa_cmp_w2, mla_q_norm, mla_kv_norm, mla_w_uq, mla_w_uk, mla_w_uv, w_br, w_o, ln1_g, ln1_b, w_ff1, w_ff3, w_ff2, ln2_g, ln2_b):
    w = dict(w_ada=w_ada, b_ada=b_ada, w_in=w_in, nsa_cmp_pe=nsa_cmp_pe, nsa_cmp_w1=nsa_cmp_w1, nsa_cmp_w2=nsa_cmp_w2,
             mla_q_norm=mla_q_norm, mla_kv_norm=mla_kv_norm, mla_w_uq=mla_w_uq, mla_w_uk=mla_w_uk,
             mla_w_uv=mla_w_uv, w_br=w_br, w_o=w_o, ln1_g=ln1_g, ln1_b=ln1_b, w_ff1=w_ff1, w_ff3=w_ff3,
             w_ff2=w_ff2, ln2_g=ln2_g, ln2_b=ln2_b, ln_in_g=ln_in_g, ln_in_b=ln_in_b, rel_bias=rel_bias)
    packs = [_pack_layer(l, w_in, mla_w_uq, mla_w_uk, mla_w_uv, w_br, nsa_cmp_pe, nsa_cmp_w1, nsa_cmp_w2)
             for l in range(DEPTH)]
    tabs = _bias_tables(rel_bias)
    y_p, st_p = _prompt_trunk(x_prompt, c_prompt, w, packs, tabs)
    pools = (cache_sb_kv, cache_nsa_kv, state_nsa_win, cache_mla, cache_moba_kv)
    y_s, st_s = _decode_trunk(x_sample, c_sample, pools, page_table, w, packs, tabs)
    out = [y_p, y_s]
    for a, b in zip(st_p, st_s):
        out += [a, b]
    return tuple(out)


def _decode_trunk(x, c, pools, page_table, w, packs, tabs):
    n_seq, tt, _ = x.shape
    assert tt <= 8 and tt & (tt - 1) == 0
    sb_pool, nsa_pool, win_state, mla_pool, moba_pool = pools
    n_pool = sb_pool.shape[1]
    n_pages = page_table.shape[1]
    past_len = n_pages * PAGE
    assert past_len % MOBA_BLOCK == 0 and past_len >= WINDOW and past_len // SEL_BLOCK + 1 <= 256
    n_chunks = n_pages // G_PAGES
    rows = 8 * N_HEADS
    lut, _ = tabs
    far_nsa = jnp.repeat(w["rel_bias"][NUM_BUCKETS - 1, 0:N_HEADS], 8).reshape(rows, 1)
    far_moba = jnp.repeat(w["rel_bias"][NUM_BUCKETS - 1, N_HEADS:2 * N_HEADS], 8).reshape(rows, 1)
    feat_major = lambda p: jnp.transpose(p, (0, 1, 3, 4, 5, 2)).reshape(DEPTH, n_pool, 256, PAGE)
    sb_pool = feat_major(sb_pool)
    nsa_pool4 = feat_major(nsa_pool)
    moba_pool = feat_major(moba_pool)
    mla_pool = jnp.transpose(mla_pool, (0, 1, 3, 2))
    win_state = win_state.reshape(DEPTH, n_seq, win_state.shape[2], SLAB)
    r = n_seq * tt
    xf = x.reshape(1, r, D_MODEL)
    per_seq = lambda a: a.reshape(n_seq, tt, a.shape[-1])
    new = ([], [], [], [], [])
    for l in range(DEPTH):
        pk = packs[l]
        mod = _ada(c, w["w_ada"][l], w["b_ada"][l])
        mod = jnp.repeat(mod, tt, axis=0).reshape(1, r, 6 * D_MODEL)
        res = _proj(xf, mod, w["ln_in_g"], w["ln_in_b"], pk, w["mla_q_norm"][l], w["mla_kv_norm"][l],
                    apply_ln=(l == 0), pos0=past_len, period=tt, mod_per_row=True, emit_kbar=False)
        (sbq, sbkv, sbkvb, nsaq, nsakv, nsawin, nsakvb, nsag, mlar, mlarb, mlaq, mobaq, mobakv, mobakvb) = res[:14]
        if l == 0:
            xf = res[14]
        o_sb = _sb_dec(page_table, sb_pool, l, per_seq(sbq), per_seq(sbkvb))
        o_mla = _mla_dec(page_table, mla_pool, l, per_seq(mlaq), per_seq(mlarb), pk["w_uv"])
        o_moba = _moba_dec(page_table, moba_pool, l, per_seq(mobaq), per_seq(mobakvb), lut, far_moba)
        fs = _cmp_a_dec(page_table, nsa_pool4, l, pk["w_c1d"])
        cmp = _cmp_b(fs, pk["pe_rows"], pk["w_c1"], pk["w_c2"])
        o_nsa = _nsa_dec(page_table, nsa_pool4, l, per_seq(nsaq), per_seq(nsakvb), cmp, per_seq(nsag), win_state[l],
                         lut, far_nsa)
        flat = lambda o: o.reshape(1, r, 512)
        xf = _dense_tail(l, xf, mod, True, (flat(o_sb), flat(o_nsa), flat(o_mla), flat(o_moba)), w, pk)
        wl = win_state.shape[2]
        win_new = jnp.concatenate([win_state[l], per_seq(nsawin)], axis=1)[:, tt:]
        rows_out = (sbkv.reshape(n_seq, tt, 2, 2, HD), nsakv.reshape(n_seq, tt, 2, 2, HD),
                    win_new.reshape(n_seq, wl, 2, HD), per_seq(mlar), mobakv.reshape(n_seq, tt, 2, 2, HD))
        for lst, a in zip(new, rows_out):
            lst.append(a)
    return xf.reshape(n_seq, tt, D_MODEL), [jnp.stack(lst, axis=0) for lst in new]


G_PAGES = 32
CHUNK_PITCH = 24


def _page_specs(l, feats, n_chunks, reverse, feat_block=0):
    def spec(g):
        if reverse:
            return pl.BlockSpec((None, None, feats, PAGE),
                                lambda n, c, pt: (l, pt[n, (n_chunks - 1 - c) * G_PAGES + g], feat_block, 0))
        return pl.BlockSpec((None, None, feats, PAGE), lambda n, c, pt: (l, pt[n, c * G_PAGES + g], feat_block, 0))
    return [spec(g) for g in range(G_PAGES)]


def _seq_spec(shape):
    nd = len(shape)
    return pl.BlockSpec((1,) + tuple(shape[1:]), lambda n, c, pt: (n,) + (0,) * (nd - 1))


def _const_spec(a):
    return pl.BlockSpec(a.shape, lambda n, c, pt: (0,) * a.ndim)


def _paged_call(kernel, name, page_table, cache, l, feats, reverse, seq_ins, const_ins, out_lanes, scratch,
                feat_block=0):
    n_seq, n_pages = page_table.shape
    assert n_pages % G_PAGES == 0
    n_chunks = n_pages // G_PAGES
    tt = seq_ins[0].shape[1]
    grid_spec = pltpu.PrefetchScalarGridSpec(
        num_scalar_prefetch=1, grid=(n_seq, n_chunks),
        in_specs=[_seq_spec(a.shape) for a in seq_ins] + [_const_spec(a) for a in const_ins]
        + _page_specs(l, feats, n_chunks, reverse, feat_block),
        out_specs=_seq_spec((n_seq, tt, out_lanes)),
        scratch_shapes=scratch)
    return pl.pallas_call(
        functools.partial(kernel, n_seq_ins=len(seq_ins), n_const=len(const_ins)),
        grid_spec=grid_spec, out_shape=jax.ShapeDtypeStruct((n_seq, tt, out_lanes), F32),
        compiler_params=_cparams(("parallel", "arbitrary")), name=name,
    )(page_table, *seq_ins, *const_ins, *([cache] * G_PAGES))


def _fill_rows(dst_ref, src, width, lane_of_head):
    dst_ref[...] = jnp.zeros(dst_ref.shape, F32)
    tt = src.shape[0]
    for h in range(N_HEADS):
        lo = lane_of_head(h)
        dst_ref[8 * h:8 * h + tt, lo:lo + width] = src[:, width * h:width * (h + 1)].astype(F32)


def _fill_new(dst_ref, src):
    dst_ref[...] = jnp.zeros(dst_ref.shape, F32)
    dst_ref[0:src.shape[0], :] = src.astype(F32)


def _row_t(shape):
    return lax.broadcasted_iota(jnp.int32, shape, 0) & 7


def _head_lut(lut_ref, head0, idx):
    return jnp.concatenate([_lut_gather(lut_ref[head0 + h:head0 + h + 1, :], idx[8 * h:8 * h + 8])
                            for h in range(N_HEADS)], axis=0)


def _sb_dec_kernel(pt_ref, *refs, n_seq_ins, n_const):
    q_ref, new_ref, tri_ref = refs[:3]
    pages = refs[3:3 + G_PAGES]
    o_ref, q32_ref, new32_ref, acc_ref, carry_ref = refs[3 + G_PAGES:]
    c = pl.program_id(1)
    tt = q_ref.shape[1]
    tri = tri_ref[...]
    rt = _row_t((8 * N_HEADS, PAGE))
    col = lax.broadcasted_iota(jnp.int32, (8 * N_HEADS, PAGE), 1)

    @pl.when(c == 0)
    def _():
        _fill_rows(q32_ref, q_ref[0], SLAB, lambda h: SLAB * (h // 2))
        _fill_new(new32_ref, new_ref[0])
        kvt = new32_ref[...].astype(BF16)
        z = _dot_nt(q32_ref[...].astype(BF16), kvt)
        contrib, carry = _sb_tile(z, kvt, tri, jnp.zeros((8 * N_HEADS, 1), F32), col < rt)
        acc_ref[...] = contrib
        carry_ref[...] = carry

    q = q32_ref[...].astype(BF16)
    kv = [pages[g][...].astype(BF16) for g in range(G_PAGES)]
    zs = [_dot(q, kv[g]) for g in range(G_PAGES)]
    lrs = [_neg_softplus(z) for z in zs]
    tails = _split_dot(jnp.concatenate(lrs, axis=0), tri)
    carry = carry_ref[...]
    ws = [None] * G_PAGES
    for g in reversed(range(G_PAGES)):
        tail = tails[8 * N_HEADS * g:8 * N_HEADS * (g + 1)]
        ws[g] = jnp.exp(zs[g] + lrs[g] + tail + carry).astype(BF16)
        carry = carry + tail[:, 0:1] + lrs[g][:, 0:1]
    carry_ref[...] = carry
    acc_ref[...] += _dot_nt(jnp.concatenate(ws, axis=1), jnp.concatenate(kv, axis=1))

    @pl.when(c == pl.num_programs(1) - 1)
    def _():
        acc = acc_ref[...]
        for h in range(N_HEADS):
            o_ref[0, :, SLAB * h:SLAB * (h + 1)] = acc[8 * h:8 * h + tt, SLAB * (h // 2):SLAB * (h // 2 + 1)]


def _mla_dec_kernel(pt_ref, *refs, n_seq_ins, n_const):
    q_ref, new_ref, wuv_ref = refs[:3]
    pages = refs[3:3 + G_PAGES]
    o_ref, q32_ref, new32_ref, m_ref, l_ref, acc_ref = refs[3 + G_PAGES:]
    c = pl.program_id(1)
    tt = q_ref.shape[1]
    rt = _row_t((8 * N_HEADS, PAGE))
    col = lax.broadcasted_iota(jnp.int32, (8 * N_HEADS, PAGE), 1)

    @pl.when(c == 0)
    def _():
        _fill_rows(q32_ref, q_ref[0], 256, lambda h: 0)
        _fill_new(new32_ref, new_ref[0])
        rows = new32_ref[...].astype(BF16)
        s = jnp.where((col <= rt) & (col < tt), _dot_nt(q32_ref[...].astype(BF16), rows), NEG)
        _softmax_step(s, rows[:, 0:KV_LORA], m_ref, l_ref, acc_ref, True)

    q = q32_ref[...].astype(BF16)
    ss, vs = [], []
    for g in range(G_PAGES):
        lat = pages[g][0:KV_LORA, :].astype(BF16)
        ss.append(_dot(q[:, 0:KV_LORA], lat)
                  + _dot(q[:, KV_LORA:KV_LORA + ROPE], pages[g][KV_LORA:KV_LORA + ROPE, :].astype(BF16)))
        vs.append(lat)
    _softmax_step(jnp.concatenate(ss, axis=1), jnp.concatenate(vs, axis=1), m_ref, l_ref, acc_ref, False,
                  v_is_transposed=True)

    @pl.when(c == pl.num_programs(1) - 1)
    def _():
        olat = (acc_ref[...] / jnp.maximum(l_ref[...], 1e-30)).astype(BF16)
        for h in range(N_HEADS):
            o_ref[0, :, SLAB * h:SLAB * (h + 1)] = _dot(olat[8 * h:8 * h + 8], wuv_ref[h])[0:tt]


def _moba_dec_kernel(pt_ref, *refs, n_seq_ins, n_const, past_len):
    q_ref, new_ref, lut_ref, far_ref = refs[:4]
    pages = refs[4:4 + G_PAGES]
    (o_ref, q32_ref, new32_ref, gs_ref, mb_ref, lb_ref, accb_ref, blast_ref) = refs[4 + G_PAGES:]
    c = pl.program_id(1)
    n_chunks = gs_ref.shape[0]
    bpc = G_PAGES // 2
    tt = q_ref.shape[1]
    rows = 8 * N_HEADS
    rt = _row_t((rows, PAGE))
    col = lax.broadcasted_iota(jnp.int32, (rows, PAGE), 1)
    far = far_ref[...]

    @pl.when(c == 0)
    def _():
        _fill_rows(q32_ref, q_ref[0], SLAB, lambda h: SLAB * (h // 2))
        _fill_new(new32_ref, new_ref[0])
        gs_ref[...] = jnp.zeros(gs_ref.shape, F32)
        mb_ref[...] = jnp.zeros(mb_ref.shape, F32)
        lb_ref[...] = jnp.zeros(lb_ref.shape, F32)
        t2 = _row_t((rows, MOBA_BLOCK))
        s2 = lax.broadcasted_iota(jnp.int32, (rows, MOBA_BLOCK), 1)
        blast_ref[...] = _head_lut(lut_ref, N_HEADS, jnp.clip(MOBA_BLOCK + t2 - s2, 0, 127))

    q = q32_ref[...].astype(BF16)
    for k in range(bpc):
        kv = jnp.concatenate([pages[2 * k][...], pages[2 * k + 1][...]], axis=1).astype(BF16)
        is_last = (c == n_chunks - 1) & (k == bpc - 1)
        raw = _dot(q, kv)
        s = raw + jnp.where(is_last, blast_ref[...], far)
        m = jnp.max(s, axis=1, keepdims=True)
        p = jnp.exp(s - m)
        gs_ref[c, :, k:k + 1] = jnp.sum(raw, axis=1, keepdims=True) * (1.0 / MOBA_BLOCK)
        mb_ref[c, :, k:k + 1] = m
        lb_ref[c, :, k:k + 1] = jnp.sum(p, axis=1, keepdims=True)
        accb_ref[c * bpc + k] = _dot_nt(p.astype(BF16), kv)

    @pl.when(c == n_chunks - 1)
    def _():
        lane = lax.broadcasted_iota(jnp.int32, (rows, 128), 1)

        def gather(ref):
            out = jnp.zeros((rows, 128), F32)
            for cc in range(n_chunks):
                part = jnp.where(lane < bpc, ref[cc], 0.0)
                out = out + (part if cc == 0 else pltpu.roll(part, cc * bpc, 1))
            return out

        nblk = n_chunks * bpc
        gs = jnp.where(lane < nblk, gather(gs_ref), NEG)
        ids = _topk_ids(gs, MOBA_TOP)
        sel = (lane == ids[0]) | (lane == ids[1]) | (lane == ids[2])
        m_all = gather(mb_ref)
        l_all = gather(lb_ref)
        kvn = new32_ref[...].astype(BF16)
        bias_n = _head_lut(lut_ref, N_HEADS, jnp.clip(rt - col, 0, 127))
        s_n = jnp.where((col <= rt) & (col < tt), _dot_nt(q, kvn) + bias_n, NEG)
        m_n = jnp.max(s_n, axis=1, keepdims=True)
        m_fin = jnp.maximum(m_n, jnp.max(jnp.where(sel, m_all, NEG), axis=1, keepdims=True))
        p_n = jnp.exp(s_n - m_fin)
        wgt = jnp.where(sel, jnp.exp(m_all - m_fin), 0.0)
        l_fin = jnp.sum(p_n, axis=1, keepdims=True) + jnp.sum(wgt * l_all, axis=1, keepdims=True)
        acc = _dot(p_n.astype(BF16), kvn)
        for b in range(nblk):
            acc = acc + wgt[:, b:b + 1] * accb_ref[b]
        out = acc / jnp.maximum(l_fin, 1e-30)
        for h in range(N_HEADS):
            o_ref[0, :, SLAB * h:SLAB * (h + 1)] = out[8 * h:8 * h + tt, SLAB * (h // 2):SLAB * (h // 2 + 1)]


def _cmp_a_dec_kernel(pt_ref, *refs):
    w_ref = refs[0]
    pages = refs[1:1 + G_PAGES]
    o_ref, xt_ref = refs[1 + G_PAGES:]
    chunks = []
    n_chunk = PAGE // CMP_STRIDE
    for g in range(G_PAGES):
        xt = pages[g][...].T
        for j in range(n_chunk):
            xt_ref[CHUNK_PITCH * j:CHUNK_PITCH * j + CMP_STRIDE, :] = xt[CMP_STRIDE * j:CMP_STRIDE * (j + 1), :]
        chunks.append(jnp.concatenate([xt_ref[pl.ds(s, n_chunk, stride=CHUNK_PITCH), :]
                                       for s in range(CMP_STRIDE)], axis=1))
    o_ref[0] = _dot(jnp.concatenate(chunks, axis=0).astype(BF16), w_ref[...])


def _cmp_a_dec(page_table, pool_t, l, w_c1d):
    n_seq, n_pages = page_table.shape
    n_chunks = n_pages // G_PAGES
    rows = PAGE // CMP_STRIDE
    grid_spec = pltpu.PrefetchScalarGridSpec(
        num_scalar_prefetch=1, grid=(n_seq, n_chunks),
        in_specs=[_const_spec(w_c1d)]
        + [pl.BlockSpec((None, None, SLAB, PAGE), lambda n, c, pt, g=g: (l, pt[n, c * G_PAGES + g], 0, 0))
           for g in range(G_PAGES)],
        out_specs=pl.BlockSpec((1, rows * G_PAGES, 512), lambda n, c, pt: (n, c, 0)),
        scratch_shapes=[pltpu.VMEM((CHUNK_PITCH * rows, SLAB), F32)])
    return pl.pallas_call(
        _cmp_a_dec_kernel, grid_spec=grid_spec,
        out_shape=jax.ShapeDtypeStruct((n_seq, rows * n_pages, 512), F32),
        compiler_params=_cparams(("parallel", "arbitrary")), name="cmp_a_dec",
    )(page_table, w_c1d, *([pool_t] * G_PAGES))


def _nsa_dec_kernel(pt_ref, *refs, n_seq_ins, n_const, past_len):
    q_ref, new_ref, cmp_ref, g_ref, win_ref, lut_ref, cover_ref, far_ref = refs[:8]
    pages = refs[8:8 + G_PAGES]
    (o_ref, q32_ref, newsel_ref, newwin_ref, g32_ref, sel_ref, blast_ref, m_ref, l_ref, acc_ref,
     oc_ref, ow_ref) = refs[8 + G_PAGES:]
    c = pl.program_id(1)
    n_chunks = sel_ref.shape[0]
    n_pages = n_chunks * G_PAGES
    tt = q_ref.shape[1]
    nh = N_HEADS
    rows = 8 * nh
    rt = _row_t((rows, PAGE))
    col = lax.broadcasted_iota(jnp.int32, (rows, PAGE), 1)
    far = far_ref[...]
    rep = lambda x: jnp.concatenate([x] * nh, axis=0)

    @pl.when(c == 0)
    def _():
        _fill_rows(q32_ref, q_ref[0], SLAB, lambda h: 0)
        new = new_ref[0]
        _fill_new(newsel_ref, new[:, SLAB:2 * SLAB])
        _fill_new(newwin_ref, new[:, 2 * SLAB:3 * SLAB])
        g32_ref[...] = jnp.zeros(g32_ref.shape, F32)
        for h in range(nh):
            g32_ref[8 * h:8 * h + tt, :] = g_ref[0]
        q = q32_ref[...].astype(BF16)
        cmp = cmp_ref[0]
        j1 = cmp.shape[0]
        jcol = lax.broadcasted_iota(jnp.int32, (rows, j1), 1)
        rel_c = past_len + _row_t((rows, j1)) - (jcol * CMP_STRIDE + (CMP_LEN - 1))
        valid = jcol < j1 - 1
        bias = _head_lut(lut_ref, 0, jnp.clip(rel_c, 0, 127))
        pc = _masked_softmax(jnp.where(valid, _dot_nt(q, cmp) + bias, NEG), valid)
        oc_ref[...] = _dot(pc.astype(BF16), cmp)
        psum = pc[0:8] + pc[8:16] + pc[16:24] + pc[24:32]
        psum = jnp.concatenate([psum, jnp.zeros((128 - 8, j1), F32)], axis=0)
        p_hi = psum.astype(BF16)
        p_lo = (psum - p_hi.astype(F32)).astype(BF16)
        imp_t = _dot_nt(cover_ref[...], p_hi) + _dot_nt(cover_ref[...], p_lo)
        qpos = past_len + lax.broadcasted_iota(jnp.int32, (1, 128), 1)
        sel = _select_blocks_t(imp_t, qpos).T[0:8]
        per = 2 * G_PAGES
        sel_ref[...] = jnp.zeros(sel_ref.shape, F32)
        for cc in range(n_chunks):
            sel_ref[cc, :, 0:per] = sel[:, cc * per:(cc + 1) * per]
        blast_ref[...] = _head_lut(lut_ref, 0, jnp.clip(PAGE + rt - col, 0, 127))
        wst = win_ref[0].astype(BF16)
        wl = wst.shape[0]
        tw = _row_t((rows, wl))
        iw = lax.broadcasted_iota(jnp.int32, (rows, wl), 1)
        s = _dot_nt(q, wst) + _head_lut(lut_ref, 0, jnp.clip(wl + tw - iw, 0, 127))
        _softmax_step(jnp.where(iw > tw, s, NEG), wst, m_ref, l_ref, acc_ref, True)
        bias_n = _head_lut(lut_ref, 0, jnp.clip(rt - col, 0, 127))
        ok_n = (col <= rt) & (col < tt)
        wn = newwin_ref[...].astype(BF16)
        _softmax_step(jnp.where(ok_n, _dot_nt(q, wn) + bias_n, NEG), wn, m_ref, l_ref, acc_ref, False)
        ow_ref[...] = acc_ref[...] / jnp.maximum(l_ref[...], 1e-30)
        sn = newsel_ref[...].astype(BF16)
        _softmax_step(jnp.where(ok_n, _dot_nt(q, sn) + bias_n, NEG), sn, m_ref, l_ref, acc_ref, True)

    q = q32_ref[...].astype(BF16)
    selc = sel_ref[c]
    ss, vs = [], []
    for g in range(G_PAGES):
        kv = pages[g][...].astype(BF16)
        chosen = jnp.where(col[0:8] < SEL_BLOCK, selc[:, 2 * g:2 * g + 1], selc[:, 2 * g + 1:2 * g + 2]) > 0.5
        is_last = (c == n_chunks - 1) & (g == G_PAGES - 1)
        s = _dot(q, kv) + jnp.where(is_last, blast_ref[...], far)
        ss.append(jnp.where(rep(chosen), s, NEG))
        vs.append(kv)
    _softmax_step(jnp.concatenate(ss, axis=1), jnp.concatenate(vs, axis=1), m_ref, l_ref, acc_ref, False,
                  v_is_transposed=True)

    @pl.when(c == n_chunks - 1)
    def _():
        osel = acc_ref[...] / jnp.maximum(l_ref[...], 1e-30)
        lane = lax.broadcasted_iota(jnp.int32, (rows, 128), 1)
        head = lax.broadcasted_iota(jnp.int32, (rows, 128), 0) >> 3
        g32 = g32_ref[...]
        gcol = lambda k: jnp.sum(jnp.where(lane == k * nh + head, g32, 0.0), axis=1, keepdims=True)
        out = gcol(0) * oc_ref[...] + gcol(1) * osel + gcol(2) * ow_ref[...]
        for h in range(nh):
            o_ref[0, :, SLAB * h:SLAB * (h + 1)] = out[8 * h:8 * h + tt]


_ROWS = 8 * N_HEADS


def _sb_dec(page_table, pool_t, l, q, new):
    vm = pltpu.VMEM
    return _paged_call(
        _sb_dec_kernel, "sb_dec", page_table, pool_t, l, 256, True, [q, new], [_tri(PAGE)], 512,
        [vm((_ROWS, 256), F32), vm((PAGE, 256), F32), vm((_ROWS, 256), F32), vm((_ROWS, 1), F32)])


def _mla_dec(page_table, pool_t, l, q, new, w_uv):
    vm = pltpu.VMEM
    return _paged_call(
        _mla_dec_kernel, "mla_dec", page_table, pool_t, l, KV_LORA + ROPE, False, [q, new], [w_uv], 512,
        [vm((_ROWS, 256), F32), vm((PAGE, 256), F32), vm((_ROWS, 1), F32), vm((_ROWS, 1), F32),
         vm((_ROWS, KV_LORA), F32)])


def _moba_dec(page_table, pool_t, l, q, new, lut, far):
    vm = pltpu.VMEM
    n_pages = page_table.shape[1]
    n_chunks = n_pages // G_PAGES
    return _paged_call(
        functools.partial(_moba_dec_kernel, past_len=n_pages * PAGE), "moba_dec", page_table, pool_t, l, 256, False,
        [q, new], [lut, far], 512,
        [vm((_ROWS, 256), F32), vm((PAGE, 256), F32)] + [vm((n_chunks, _ROWS, 128), F32)] * 3
        + [vm((n_pages // 2, _ROWS, 256), F32), vm((_ROWS, MOBA_BLOCK), F32)])


def _nsa_dec(page_table, pool_t, l, q, new, cmp, gates, win_state, lut, far):
    vm = pltpu.VMEM
    n_pages = page_table.shape[1]
    n_chunks = n_pages // G_PAGES
    past_len = n_pages * PAGE
    cover = _cover(past_len // CMP_STRIDE, 256).T
    return _paged_call(
        functools.partial(_nsa_dec_kernel, past_len=past_len), "nsa_dec", page_table, pool_t, l, SLAB, False,
        [q, new, cmp, gates, win_state], [lut, cover, far], 512,
        [vm((_ROWS, SLAB), F32), vm((PAGE, SLAB), F32), vm((PAGE, SLAB), F32), vm((_ROWS, 128), F32),
         vm((n_chunks, 8, 128), F32), vm((_ROWS, PAGE), F32), vm((_ROWS, 1), F32), vm((_ROWS, 1), F32),
         vm((_ROWS, SLAB), F32), vm((_ROWS, SLAB), F32), vm((_ROWS, SLAB), F32)],
        feat_block=1)


def _dense_tail(l, x, mod, mod_per_row, outs, w, pk):
    row = lambda a: a[l].reshape(1, -1)
    x = _row_call(_merge_kernel, "merge", x, mod, mod_per_row, list(outs),
                  [pk["w_gate"], pk["w_brp"], w["w_o"][l].astype(BF16), row(w["ln1_g"]), row(w["ln1_b"])])
    return _row_call(_ffn_kernel, "ffn", x, mod, mod_per_row, [],
                     [w["w_ff1"][l].astype(BF16), w["w_ff3"][l].astype(BF16), w["w_ff2"][l].astype(BF16),
                      row(w["ln2_g"]), row(w["ln2_b"])])


def _prompt_trunk(x, c, w, packs, tabs):
    nb, t, _ = x.shape
    lut, tiles = tabs
    bt_nsa, far_nsa = _group_tiles(tiles[0:4], w["rel_bias"][NUM_BUCKETS - 1, 0:4], 1)
    bt_moba, far_moba = _group_tiles(tiles[4:8], w["rel_bias"][NUM_BUCKETS - 1, 4:8], 2)
    new = ([], [], [], [], [])
    for l in range(DEPTH):
        pk = packs[l]
        mod = _ada(c, w["w_ada"][l], w["b_ada"][l]).reshape(nb, 1, 6 * D_MODEL)
        res = _proj(x, mod, w["ln_in_g"], w["ln_in_b"], pk, w["mla_q_norm"][l], w["mla_kv_norm"][l],
                    apply_ln=(l == 0), pos0=0, period=t, mod_per_row=False, emit_kbar=True)
        (sbq, sbkv, sbkvb, nsaq, nsakv, nsawin, nsakvb, nsag, mlar, mlarb, mlaq, mobaq, mobakv, mobakvb) = res[:14]
        if l == 0:
            x = res[14]
        kbar = res[-1].reshape(nb, t // TILE, 256)
        kbar = jnp.pad(kbar, ((0, 0), (0, 128 - kbar.shape[1]), (0, 0)))
        o_sb = _sb_prompt(sbq, sbkvb)
        fs = _cmp_a(nsakv.reshape(nb, t // CMP_STRIDE, CMP_STRIDE * 256), pk["w_c1"])
        cmp = _cmp_b(fs, pk["pe_rows"], pk["w_c1"], pk["w_c2"])
        o_nsa = _nsa_prompt(nsaq, nsakvb, cmp, nsag, lut[0:4], bt_nsa[0], far_nsa[0])
        o_mla = _mla_prompt(mlaq, mlarb, pk["w_uv"])
        o_moba = _moba_prompt(mobaq, mobakvb, kbar, bt_moba, far_moba)
        x = _dense_tail(l, x, mod, False, (o_sb, o_nsa, o_mla, o_moba), w, pk)
        wl = min(WINDOW, t)
        rows = (sbkv.reshape(nb, t, 2, 2, HD), nsakv.reshape(nb, t, 2, 2, HD),
                nsawin[:, t - wl:].reshape(nb, wl, 2, HD), mlar, mobakv.reshape(nb, t, 2, 2, HD))
        for lst, r in zip(new, rows):
            lst.append(r)
    return x, [jnp.stack(lst, axis=0) for lst in new]
```

```python
import functools
import math

import numpy as np
import jax
import jax.numpy as jnp
from jax import lax
from jax.experimental import pallas as pl
from jax.experimental.pallas import tpu as pltpu

F32 = jnp.float32
BF16 = jnp.bfloat16

D_MODEL = 1024
PAGE = 128
HD = 64
SLAB = 2 * HD
N_HEADS = 4
CMP_STRIDE = 16
CMP_LEN = 32
CMP_HIDDEN = 128
SEL_BLOCK = 64
SEL_TOP = 16
WINDOW = 512
Q_LORA = 256
KV_LORA = 128
NOPE = 64
ROPE = 32
ROPE_THETA = 10000.0
MOBA_BLOCK = 256
MOBA_TOP = 3
NUM_BUCKETS = 32
MAX_DISTANCE = 128
DEPTH = 2
D_FF = 2816
ALPHA = (2 * DEPTH) ** 0.25
N_IN_MIX = 2092
TILE = 256
NEG = -1e30
VMEM_LIMIT = 56 * 1024 * 1024

C_SBQ, C_SBKV, C_NSAQ, C_NSAKV, C_NSAG, C_QA, C_KVA, C_KR, C_KRS, C_MOBAQ, C_MOBAKV, C_END = (
    0, 512, 768, 1280, 1664, 1792, 2048, 2176, 2304, 2432, 2944, 3200)


def _cparams(sem, vmem=VMEM_LIMIT):
    return pltpu.CompilerParams(dimension_semantics=sem, vmem_limit_bytes=vmem)


def _dot(a, b):
    return jnp.dot(a, b, preferred_element_type=F32)


def _dot_nt(a, b):
    return lax.dot_general(a, b, (((1,), (1,)), ((), ())), preferred_element_type=F32)


def _split_dot(x, w):
    hi = x.astype(BF16)
    lo = (x - hi.astype(F32)).astype(BF16)
    return _dot(hi, w) + _dot(lo, w)


def _layer_norm(x, g, b):
    mu = jnp.mean(x, axis=-1, keepdims=True)
    xc = x - mu
    var = jnp.mean(xc * xc, axis=-1, keepdims=True)
    return xc * lax.rsqrt(var + 1e-5) * g + b


def _sigmoid(x):
    return 1.0 / (1.0 + jnp.exp(-x))


def _pad_heads(w):
    k = w.shape[1]
    return jnp.pad(w.reshape(N_HEADS, HD, k), ((0, 0), (0, HD), (0, 0))).reshape(N_HEADS * SLAB, k)


def _pad_rows(w, n):
    return jnp.pad(w, ((0, n - w.shape[0]), (0, 0)))


def _pack_layer(l, w_in, mla_w_uq, mla_w_uk, mla_w_uv, w_br, nsa_cmp_pe, nsa_cmp_w1, nsa_cmp_w2):
    wi = jnp.transpose(w_in, (2, 0, 1))[:, l, :]
    o = np.cumsum([0, 256, 256, 256, 384, 12, 256, 128, 32, 256, 256]).tolist()
    sb_q, sb_kv, nsa_q, nsa_kv, nsa_g, qa, kva, kr, moba_q, moba_kv = [wi[o[i]:o[i + 1]] for i in range(10)]
    kr_sw = jnp.concatenate([kr[ROPE // 2:], kr[:ROPE // 2]], axis=0)
    w_proj = jnp.concatenate([
        _pad_heads(sb_q), sb_kv, _pad_heads(nsa_q), nsa_kv, _pad_rows(nsa_g, 128), qa, kva,
        _pad_rows(kr, 128), _pad_rows(kr_sw, 128), _pad_heads(moba_q), moba_kv], axis=0).astype(BF16)
    w_gate = wi[N_IN_MIX:].astype(BF16)
    uq = mla_w_uq[l]
    uq_nope = uq[:, :, :NOPE].reshape(Q_LORA, N_HEADS * NOPE)
    uq_rope = uq[:, :, NOPE:]
    uq_rope_sw = jnp.concatenate([uq_rope[..., ROPE // 2:], uq_rope[..., :ROPE // 2]], axis=-1)
    place = lambda r: jnp.pad(r, ((0, 0), (0, 0), (0, 128 - ROPE))).reshape(Q_LORA, N_HEADS * 128)
    w_q3 = jnp.concatenate([uq_nope, place(uq_rope), place(uq_rope_sw)], axis=1).astype(BF16)
    uk = mla_w_uk[l]
    w_uk = jnp.zeros((N_HEADS * NOPE, N_HEADS * KV_LORA), F32)
    for h in range(N_HEADS):
        w_uk = w_uk.at[h * NOPE:(h + 1) * NOPE, h * KV_LORA:(h + 1) * KV_LORA].set(uk[:, h, :].T)
    w_uk = w_uk.astype(BF16)
    uv = mla_w_uv[l]
    w_uv = jnp.pad(jnp.transpose(uv, (1, 0, 2)), ((0, 0), (0, 0), (HD, 0))).astype(BF16)
    wb = w_br[l].reshape(4, N_HEADS, HD, D_MODEL)
    w_brp = jnp.pad(wb, ((0, 0), (0, 0), (HD, 0), (0, 0))).reshape(4, N_HEADS * SLAB, D_MODEL).astype(BF16)
    w1 = nsa_cmp_w1[l].reshape(2, 2, CMP_STRIDE, HD, CMP_HIDDEN)
    halves = []
    for half in range(2):
        wh = jnp.zeros((CMP_STRIDE, 4, HD, 2, CMP_HIDDEN), F32)
        for c in range(2):
            wh = wh.at[:, c, :, c, :].set(w1[c, half])
        halves.append(wh.reshape(CMP_STRIDE * 4 * HD, 2 * CMP_HIDDEN))
    w_c1 = jnp.concatenate(halves, axis=1).astype(BF16)
    w_c1d = jnp.concatenate(halves, axis=1).reshape(CMP_STRIDE, 4, HD, 4 * CMP_HIDDEN)[:, 0:2]
    w_c1d = w_c1d.reshape(CMP_STRIDE * 2 * HD, 4 * CMP_HIDDEN).astype(BF16)
    pe = nsa_cmp_pe[l].reshape(2, 2, CMP_STRIDE, HD)
    pe_rows = []
    for half in range(2):
        p = jnp.zeros((CMP_STRIDE, 4, HD), F32)
        for c in range(2):
            p = p.at[:, c, :].set(pe[c, half])
        pe_rows.append(p.reshape(1, CMP_STRIDE * 4 * HD))
    pe_rows = jnp.concatenate(pe_rows + [jnp.zeros((6, CMP_STRIDE * 4 * HD), F32)], axis=0)
    w2 = nsa_cmp_w2[l]
    w_c2 = jnp.zeros((2 * CMP_HIDDEN, 2 * HD), F32)
    for c in range(2):
        w_c2 = w_c2.at[c * CMP_HIDDEN:(c + 1) * CMP_HIDDEN, c * HD:(c + 1) * HD].set(w2[c])
    w_c2 = w_c2.astype(BF16)
    return dict(w_proj=w_proj, w_gate=w_gate, w_q3=w_q3, w_uk=w_uk, w_uv=w_uv, w_brp=w_brp,
                w_c1=w_c1, w_c1d=w_c1d, pe_rows=pe_rows, w_c2=w_c2)


def _rope_freq():
    half = ROPE // 2
    inv = ROPE_THETA ** (-np.arange(half, dtype=np.float32) / half)
    f = np.zeros((1, 128), np.float32)
    f[0, :half] = -inv
    f[0, half:ROPE] = inv
    return jnp.asarray(f)


def _ada_kernel(c_ref, w_ref, b_ref, o_ref):
    c = c_ref[...]
    act = (c * _sigmoid(c)).astype(BF16)
    o_ref[...] = _dot(act, w_ref[...].astype(BF16)) + b_ref[...]


def _ada(c, w, b):
    n = c.shape[0]
    npad = -(-n // 8) * 8
    cp = jnp.pad(c, ((0, npad - n), (0, 0)))
    cols = 6 * D_MODEL
    tn = 1536
    out = pl.pallas_call(
        _ada_kernel,
        grid=(cols // tn,),
        in_specs=[pl.BlockSpec((npad, D_MODEL), lambda j: (0, 0)),
                  pl.BlockSpec((D_MODEL, tn), lambda j: (0, j)),
                  pl.BlockSpec((1, tn), lambda j: (0, j))],
        out_specs=pl.BlockSpec((npad, tn), lambda j: (0, j)),
        out_shape=jax.ShapeDtypeStruct((npad, cols), F32),
        compiler_params=_cparams(("arbitrary",)),
        name="ada",
    )(cp, w, b.reshape(1, cols))
    return out[:n]


def _proj_kernel(x_ref, mod_ref, lng_ref, lnb_ref, w_ref, qn_ref, kvn_ref, wq3_ref, wuk_ref, freq_ref, *outs,
                 apply_ln, pos0, period, emit_kbar):
    outs = list(outs)
    (sbq_o, sbkv_o, sbkvb_o, nsaq_o, nsakv_o, nsawin_o, nsakvb_o, nsag_o, mlar_o, mlarb_o, mlaq_o,
     mobaq_o, mobakv_o, mobakvb_o) = outs[:14]
    rest = outs[14:]
    tm = x_ref.shape[1]
    x = x_ref[0]
    if apply_ln:
        x = _layer_norm(x, lng_ref[...], lnb_ref[...])
        rest.pop(0)[0] = x
    mod = mod_ref[0]
    u = (x * (1.0 + mod[:, D_MODEL:2 * D_MODEL]) + mod[:, 0:D_MODEL]).astype(BF16)

    def grp(a, b):
        return _dot_nt(u, w_ref[a:b, :])

    qscale = HD ** -0.5
    sbq_o[0] = (grp(C_SBQ, C_SBKV) * qscale).astype(BF16)
    kv = grp(C_SBKV, C_NSAQ)
    sbkv_o[0] = kv
    sbkvb_o[0] = kv.astype(BF16)
    nsaq_o[0] = (grp(C_NSAQ, C_NSAKV) * qscale).astype(BF16)
    kv = grp(C_NSAKV, C_NSAG)
    nsakv_o[0] = kv[:, 0:256]
    nsawin_o[0] = kv[:, 256:384]
    nsakvb_o[0] = kv.astype(BF16)
    nsag_o[0] = _sigmoid(grp(C_NSAG, C_QA))
    mobaq_o[0] = (grp(C_MOBAQ, C_MOBAKV) * qscale).astype(BF16)
    kv = grp(C_MOBAKV, C_END)
    mobakv_o[0] = kv
    mobakvb_o[0] = kv.astype(BF16)
    if emit_kbar:
        rest.pop(0)[0, 0] = jnp.mean(kv, axis=0, keepdims=True)

    row = lax.broadcasted_iota(jnp.int32, (tm, 1), 0) + pl.program_id(1) * tm
    pos = (pos0 + (row & (period - 1))).astype(F32)
    ang = pos * freq_ref[...]
    cs, sn = jnp.cos(ang), jnp.sin(ang)

    kva = grp(C_KVA, C_KR)
    ckv = kva * lax.rsqrt(jnp.mean(kva * kva, axis=-1, keepdims=True) + 1e-6) * kvn_ref[...]
    kr = grp(C_KR, C_KRS) * cs + grp(C_KRS, C_MOBAQ) * sn
    mlar_o[0, :, 0:KV_LORA] = ckv
    mlar_o[0, :, KV_LORA:KV_LORA + ROPE] = kr[:, 0:ROPE]
    last = lax.broadcasted_iota(jnp.int32, kr.shape, 1) == 127
    mlarb_o[0] = jnp.concatenate([ckv, jnp.where(last, 1.0, kr)], axis=1).astype(BF16)

    qa = grp(C_QA, C_KVA)
    qan = (qa * lax.rsqrt(jnp.mean(qa * qa, axis=-1, keepdims=True) + 1e-6) * qn_ref[...]).astype(BF16)
    q3 = _dot(qan, wq3_ref[...])
    qlat = _dot(q3[:, 0:256].astype(BF16), wuk_ref[...])
    mscale = (NOPE + ROPE) ** -0.5
    for h in range(N_HEADS):
        a = q3[:, 256 + 128 * h:384 + 128 * h]
        b = q3[:, 768 + 128 * h:896 + 128 * h]
        mlaq_o[0, :, 256 * h:256 * h + 128] = (qlat[:, 128 * h:128 * h + 128] * mscale).astype(BF16)
        mlaq_o[0, :, 256 * h + 128:256 * h + 256] = ((a * cs + b * sn) * mscale).astype(BF16)


def _proj(x, mod, lng, lnb, pk, qn, kvn, *, apply_ln, pos0, period, mod_per_row, emit_kbar):
    nb, r, _ = x.shape
    tm = min(TILE, r)
    grid = (nb, r // tm)
    row = lambda c, dt: (pl.BlockSpec((1, tm, c), lambda n, i: (n, i, 0)), jax.ShapeDtypeStruct((nb, r, c), dt))
    outs = [row(512, BF16), row(256, F32), row(256, BF16), row(512, BF16), row(256, F32), row(128, F32),
            row(384, BF16), row(128, F32), row(KV_LORA + ROPE, F32), row(256, BF16), row(1024, BF16),
            row(512, BF16), row(256, F32), row(256, BF16)]
    if apply_ln:
        outs.append(row(D_MODEL, F32))
    if emit_kbar:
        outs.append((pl.BlockSpec((1, 1, 1, 256), lambda n, i: (n, i, 0, 0)),
                     jax.ShapeDtypeStruct((nb, r // tm, 1, 256), F32)))
    full = lambda a: pl.BlockSpec(a.shape, lambda n, i: (0,) * a.ndim)
    if mod_per_row:
        mod_spec = pl.BlockSpec((1, tm, 6 * D_MODEL), lambda n, i: (n, i, 0))
    else:
        mod_spec = pl.BlockSpec((1, 1, 6 * D_MODEL), lambda n, i: (n, 0, 0))
    freq = _rope_freq()
    args = [x, mod, lng.reshape(1, -1), lnb.reshape(1, -1), pk["w_proj"], qn.reshape(1, -1), kvn.reshape(1, -1),
            pk["w_q3"], pk["w_uk"], freq]
    in_specs = [pl.BlockSpec((1, tm, D_MODEL), lambda n, i: (n, i, 0)), mod_spec] + [full(a) for a in args[2:]]
    res = pl.pallas_call(
        functools.partial(_proj_kernel, apply_ln=apply_ln, pos0=pos0, period=period, emit_kbar=emit_kbar),
        grid=grid, in_specs=in_specs,
        out_specs=[o[0] for o in outs], out_shape=[o[1] for o in outs],
        compiler_params=_cparams(("parallel", "parallel")),
        name="proj",
    )(*args)
    return res


def _merge_kernel(x_ref, mod_ref, o0, o1, o2, o3, wg_ref, wbr_ref, wo_ref, g_ref, b_ref, out_ref):
    x = x_ref[0]
    mod = mod_ref[0]
    u = (x * (1.0 + mod[:, D_MODEL:2 * D_MODEL]) + mod[:, 0:D_MODEL]).astype(BF16)
    merged = None
    for bi, o in enumerate((o0, o1, o2, o3)):
        gate = _sigmoid(_dot_nt(u, wg_ref[bi * D_MODEL:(bi + 1) * D_MODEL, :]))
        y = gate * _dot(o[0].astype(BF16), wbr_ref[bi])
        merged = y if merged is None else merged + y
    mix = _dot(merged.astype(BF16), wo_ref[...])
    out_ref[0] = _layer_norm(ALPHA * x + mod[:, 2 * D_MODEL:3 * D_MODEL] * mix, g_ref[...], b_ref[...])


def _ffn_kernel(x_ref, mod_ref, w1_ref, w3_ref, w2_ref, g_ref, b_ref, out_ref):
    x = x_ref[0]
    mod = mod_ref[0]
    u = (x * (1.0 + mod[:, 4 * D_MODEL:5 * D_MODEL]) + mod[:, 3 * D_MODEL:4 * D_MODEL]).astype(BF16)
    a = _dot(u, w1_ref[...])
    h = (a * _sigmoid(a) * _dot(u, w3_ref[...])).astype(BF16)
    f = _dot(h, w2_ref[...])
    out_ref[0] = _layer_norm(ALPHA * x + mod[:, 5 * D_MODEL:6 * D_MODEL] * f, g_ref[...], b_ref[...])


def _row_call(kernel, name, x, mod, mod_per_row, row_ins, consts):
    nb, r, _ = x.shape
    tm = min(TILE, r)
    if mod_per_row:
        mod_spec = pl.BlockSpec((1, tm, 6 * D_MODEL), lambda n, i: (n, i, 0))
    else:
        mod_spec = pl.BlockSpec((1, 1, 6 * D_MODEL), lambda n, i: (n, 0, 0))
    rspec = lambda a: pl.BlockSpec((1, tm, a.shape[-1]), lambda n, i: (n, i, 0))
    full = lambda a: pl.BlockSpec(a.shape, lambda n, i: (0,) * a.ndim, pipeline_mode=pl.Buffered(1))
    return pl.pallas_call(
        kernel, grid=(nb, r // tm),
        in_specs=[rspec(x), mod_spec] + [rspec(a) for a in row_ins] + [full(a) for a in consts],
        out_specs=rspec(x), out_shape=jax.ShapeDtypeStruct(x.shape, F32),
        compiler_params=_cparams(("parallel", "parallel")),
        name=name,
    )(x, mod, *row_ins, *consts)


def _neg_softplus(z):
    return -(jnp.maximum(z, 0.0) + jnp.log(1.0 + jnp.exp(-jnp.abs(z))))


def _sb_tile(z, kvt, tri, carry, mask):
    lr = _neg_softplus(z)
    if mask is not None:
        lr = jnp.where(mask, lr, 0.0)
    tail = _split_dot(lr, tri)
    w = jnp.exp(z + lr + tail + carry)
    if mask is not None:
        w = jnp.where(mask, w, 0.0)
    contrib = _dot(w.astype(BF16), kvt)
    return contrib, carry + tail[:, 0:1] + lr[:, 0:1]


def _rep2(x):
    return jnp.concatenate([x, x], axis=1)


def _sb_tile_rep(z, kvt, trij, carry, mask):
    tk = z.shape[1]
    lr = _neg_softplus(z)
    if mask is not None:
        lr = jnp.where(mask, lr, 0.0)
    tl = _split_dot(lr, trij)
    w = jnp.exp(z + lr + tl[:, 0:tk] + _rep2(carry))
    if mask is not None:
        w = jnp.where(mask, w, 0.0)
    return _dot(w.astype(BF16), kvt), carry + tl[:, tk:tk + 128]


def _sb_prompt_kernel(q_ref, kv_ref, tri_ref, o_ref, acc_ref, carry_ref):
    tq = q_ref.shape[1]
    i = pl.program_id(1)
    rowq = lax.broadcasted_iota(jnp.int32, (2 * tq, tq), 0) & (tq - 1)
    col = lax.broadcasted_iota(jnp.int32, (2 * tq, tq), 1)
    q2 = [jnp.concatenate([q_ref[0, :, SLAB * (2 * kvh):SLAB * (2 * kvh + 1)],
                           q_ref[0, :, SLAB * (2 * kvh + 1):SLAB * (2 * kvh + 2)]], axis=0) for kvh in range(2)]

    def kv_tile(j, kvh):
        return kv_ref[0, pl.ds(pl.multiple_of(j * tq, tq), tq), SLAB * kvh:SLAB * (kvh + 1)]

    for kvh in range(2):
        kvt = kv_tile(i, kvh)
        contrib, carry = _sb_tile_rep(_dot_nt(q2[kvh], kvt), kvt, tri_ref[...], jnp.zeros((2 * tq, 128), F32),
                                      col < rowq)
        acc_ref[kvh] = contrib
        carry_ref[kvh] = carry

    def past_tile(j):
        for kvh in range(2):
            kvt = kv_tile(j, kvh)
            contrib, carry = _sb_tile_rep(_dot_nt(q2[kvh], kvt), kvt, tri_ref[...], carry_ref[kvh], None)
            acc_ref[kvh] += contrib
            carry_ref[kvh] = carry

    _pair_loop(i, past_tile)
    for kvh in range(2):
        acc = acc_ref[kvh].astype(BF16)
        o_ref[0, :, SLAB * (2 * kvh):SLAB * (2 * kvh + 1)] = acc[0:tq]
        o_ref[0, :, SLAB * (2 * kvh + 1):SLAB * (2 * kvh + 2)] = acc[tq:2 * tq]


def _tri(n, with_total=False):
    r = np.arange(n)
    t = (r[:, None] > r[None, :]).astype(np.float32)
    if with_total:
        t = np.concatenate([t, np.ones((n, 128), np.float32)], axis=1)
    return jnp.asarray(t, dtype=BF16)


def _sb_prompt(q, kvb):
    nb, t, _ = q.shape
    tq = TILE
    return pl.pallas_call(
        _sb_prompt_kernel, grid=(nb, t // tq),
        in_specs=[pl.BlockSpec((1, tq, 512), lambda n, i: (n, i, 0)),
                  pl.BlockSpec((1, t, 256), lambda n, i: (n, 0, 0)),
                  pl.BlockSpec((tq, tq + 128), lambda n, i: (0, 0))],
        out_specs=pl.BlockSpec((1, tq, 512), lambda n, i: (n, i, 0)),
        out_shape=jax.ShapeDtypeStruct((nb, t, 512), BF16),
        scratch_shapes=[pltpu.VMEM((2, 2 * tq, SLAB), F32), pltpu.VMEM((2, 2 * tq, 128), F32)],
        compiler_params=_cparams(("parallel", "arbitrary")),
        name="sb_prompt",
    )(q, kvb, _tri(tq, with_total=True))


def _softmax_step(s, vt, m_ref, l_ref, acc_ref, first, v_is_transposed=False):
    pv = _dot_nt if v_is_transposed else _dot
    if first:
        m_new = jnp.max(s, axis=1, keepdims=True)
        p = jnp.exp(s - m_new)
        l_ref[...] = jnp.sum(p, axis=1, keepdims=True)
        acc_ref[...] = pv(p.astype(BF16), vt)
    else:
        m_old = m_ref[...]
        m_new = jnp.maximum(m_old, jnp.max(s, axis=1, keepdims=True))
        a = jnp.exp(m_old - m_new)
        p = jnp.exp(s - m_new)
        l_ref[...] = a * l_ref[...] + jnp.sum(p, axis=1, keepdims=True)
        acc_ref[...] = a * acc_ref[...] + pv(p.astype(BF16), vt)
    m_ref[...] = m_new


def _pair_loop(n, tile_fn):
    @pl.when(n % 2 == 1)
    def _():
        tile_fn(n - 1)

    def body(k, _):
        j = n - (n % 2) - 1 - 2 * k
        tile_fn(j)
        tile_fn(j - 1)
        return 0

    lax.fori_loop(0, n // 2, body, 0)


def _lane_max(s):
    return jnp.maximum(s[:, 0:128], s[:, 128:256])


def _row_max_rep(mrun):
    return jnp.broadcast_to(jnp.max(mrun, axis=1, keepdims=True), mrun.shape)


def _mla_prompt_kernel(q_ref, rows_ref, wuv_ref, o_ref, m_ref, acc_ref):
    tq = q_ref.shape[1]
    i = pl.program_id(1)
    q4 = jnp.concatenate([q_ref[0, :, 256 * h:256 * (h + 1)] for h in range(N_HEADS)], axis=0)
    rowq = lax.broadcasted_iota(jnp.int32, (N_HEADS * tq, tq), 0) & (tq - 1)
    col = lax.broadcasted_iota(jnp.int32, (N_HEADS * tq, tq), 1)

    def rows_tile(j):
        return rows_ref[0, pl.ds(pl.multiple_of(j * tq, tq), tq), :]

    rt = rows_tile(i)
    s_diag = jnp.where(col <= rowq, _dot_nt(q4, rt), NEG)
    m_ref[...] = _lane_max(s_diag)

    def max_tile(j):
        m_ref[...] = jnp.maximum(m_ref[...], _lane_max(_dot_nt(q4, rows_tile(j))))

    def acc_tile(j):
        rt = rows_tile(j)
        acc_ref[...] += _dot(jnp.exp(_dot_nt(q4, rt) - _rep2(m_ref[...])).astype(BF16), rt)

    _pair_loop(i, max_tile)
    m_ref[...] = _row_max_rep(m_ref[...])
    acc_ref[...] = _dot(jnp.exp(s_diag - _rep2(m_ref[...])).astype(BF16), rt)
    _pair_loop(i, acc_tile)
    olat = (acc_ref[:, 0:KV_LORA] / jnp.maximum(acc_ref[:, 255:256], 1e-30)).astype(BF16)
    for h in range(N_HEADS):
        o_ref[0, :, SLAB * h:SLAB * (h + 1)] = _dot(olat[h * tq:(h + 1) * tq], wuv_ref[h]).astype(BF16)


def _mla_prompt(q, rowsb, w_uv):
    nb, t, _ = q.shape
    tq = TILE
    return pl.pallas_call(
        _mla_prompt_kernel, grid=(nb, t // tq),
        in_specs=[pl.BlockSpec((1, tq, 1024), lambda n, i: (n, i, 0)),
                  pl.BlockSpec((1, t, 256), lambda n, i: (n, 0, 0)),
                  pl.BlockSpec((N_HEADS, 128, 128), lambda n, i: (0, 0, 0))],
        out_specs=pl.BlockSpec((1, tq, 512), lambda n, i: (n, i, 0)),
        out_shape=jax.ShapeDtypeStruct((nb, t, 512), BF16),
        scratch_shapes=[pltpu.VMEM((N_HEADS * tq, 128), F32), pltpu.VMEM((N_HEADS * tq, 256), F32)],
        compiler_params=_cparams(("parallel", "arbitrary")),
        name="mla_prompt",
    )(q, rowsb, w_uv)


def _t5_bucket(n):
    exact = NUM_BUCKETS // 2
    nf = jnp.maximum(n, 1).astype(F32)
    far = exact + (jnp.log(nf / exact) / math.log(MAX_DISTANCE / exact) * (NUM_BUCKETS - exact)).astype(jnp.int32)
    return jnp.where(n < exact, n, jnp.minimum(far, NUM_BUCKETS - 1))


def _bias_kernel(tab_ref, lut_ref, tile_ref):
    h = pl.program_id(0)
    t = tile_ref.shape[2]

    def lookup(n):
        bucket = _t5_bucket(jnp.maximum(n, 0))
        out = jnp.zeros(n.shape, F32)
        for b in range(NUM_BUCKETS):
            out = jnp.where(bucket == b, tab_ref[b, h], out)
        return out

    lut_ref[0] = lookup(lax.broadcasted_iota(jnp.int32, (8, 128), 1))
    rel = lax.broadcasted_iota(jnp.int32, (t, t), 0) - lax.broadcasted_iota(jnp.int32, (t, t), 1)
    tile_ref[0, 0] = lookup(rel)
    tile_ref[0, 1] = lookup(rel + t)


def _bias_tables(rel_bias):
    nh = rel_bias.shape[1]
    lut, tiles = pl.pallas_call(
        _bias_kernel, grid=(nh,),
        in_specs=[pl.BlockSpec(memory_space=pltpu.SMEM)],
        out_specs=[pl.BlockSpec((1, 8, 128), lambda h: (h, 0, 0)),
                   pl.BlockSpec((1, 2, TILE, TILE), lambda h: (h, 0, 0, 0))],
        out_shape=[jax.ShapeDtypeStruct((nh, 8, 128), F32), jax.ShapeDtypeStruct((nh, 2, TILE, TILE), F32)],
        compiler_params=_cparams(("arbitrary",)),
        name="bias_tables",
    )(rel_bias)
    return lut[:, 0, :], tiles


def _group_tiles(tiles, far, groups):
    h = tiles.shape[0]
    g = h // groups
    t = tiles.shape[2]
    bt = tiles.reshape(groups, g, 2, t, t).transpose(0, 2, 1, 3, 4).reshape(groups, 2, g * t, t)
    fr = jnp.broadcast_to(far.reshape(groups, g, 1, 1), (groups, g, t, 1)).reshape(groups, g * t, 1)
    return bt, fr


def _topk_ids(score, k):
    col = lax.broadcasted_iota(jnp.int32, score.shape, 1)
    ids = []
    for _ in range(k):
        m = jnp.max(score, axis=1, keepdims=True)
        first = jnp.min(jnp.where(score == m, col, 1 << 20), axis=1, keepdims=True)
        ids.append(jnp.where(m > 0.5 * NEG, first, -1))
        score = jnp.where(col == first, NEG, score)
    return ids


def _topk_ids_t(score_t, k):
    row = lax.broadcasted_iota(jnp.int32, score_t.shape, 0)
    ids = []
    for _ in range(k):
        m = jnp.max(score_t, axis=0, keepdims=True)
        first = jnp.min(jnp.where(score_t == m, row, 1 << 20), axis=0, keepdims=True)
        ids.append(jnp.where(m > 0.5 * NEG, first, -1))
        score_t = jnp.where(row == first, NEG, score_t)
    return ids


def _with_ones(kvt):
    return jnp.concatenate([kvt, jnp.ones(kvt.shape, kvt.dtype)], axis=1)


def _where_rep(chosen, s, other):
    return jnp.concatenate([jnp.where(chosen, s[:, 0:128], other), jnp.where(chosen, s[:, 128:256], other)], axis=1)


def _moba_prompt_kernel(q_ref, kv_ref, kbar_ref, bt_ref, far_ref, o_ref, m_ref, acc_ref, id_ref):
    tq = q_ref.shape[1]
    i = pl.program_id(1)
    rowq = lax.broadcasted_iota(jnp.int32, (2 * tq, tq), 0) & (tq - 1)
    col = lax.broadcasted_iota(jnp.int32, (2 * tq, tq), 1)
    brow = lax.broadcasted_iota(jnp.int32, (kbar_ref.shape[1], 2 * tq), 0)
    q2 = [jnp.concatenate([q_ref[0, :, SLAB * (2 * kvh):SLAB * (2 * kvh + 1)],
                           q_ref[0, :, SLAB * (2 * kvh + 1):SLAB * (2 * kvh + 2)]], axis=0) for kvh in range(2)]
    for kvh in range(2):
        kb = kbar_ref[0, :, SLAB * kvh:SLAB * (kvh + 1)]
        kb_hi = kb.astype(BF16)
        kb_lo = (kb - kb_hi.astype(F32)).astype(BF16)
        gs_t = _dot_nt(kb_hi, q2[kvh]) + _dot_nt(kb_lo, q2[kvh])
        ids = _topk_ids_t(jnp.where(brow < i, gs_t, NEG), MOBA_TOP)
        for r in range(MOBA_TOP):
            id_ref[kvh, r] = jnp.broadcast_to(ids[r].astype(F32), (128, 2 * tq)).T
    far2 = [jnp.broadcast_to(far_ref[kvh], (2 * tq, tq)) for kvh in range(2)]

    def kv_tile(j, kvh):
        return kv_ref[0, pl.ds(pl.multiple_of(j * tq, tq), tq), SLAB * kvh:SLAB * (kvh + 1)]

    def past_scores(j, kvh):
        kvt = kv_tile(j, kvh)
        jf = j.astype(F32)
        chosen = (id_ref[kvh, 0] == jf) | (id_ref[kvh, 1] == jf) | (id_ref[kvh, 2] == jf)
        bias = jnp.where(j == i - 1, bt_ref[kvh, 1], far2[kvh])
        return _where_rep(chosen, _dot_nt(q2[kvh], kvt) + bias, NEG), kvt

    kv_diag = [kv_tile(i, kvh) for kvh in range(2)]
    s_diag = [jnp.where(col <= rowq, _dot_nt(q2[kvh], kv_diag[kvh]) + bt_ref[kvh, 0], NEG) for kvh in range(2)]
    for kvh in range(2):
        m_ref[kvh] = _lane_max(s_diag[kvh])

    def max_tile(j):
        for kvh in range(2):
            m_ref[kvh] = jnp.maximum(m_ref[kvh], _lane_max(past_scores(j, kvh)[0]))

    _pair_loop(i, max_tile)
    for kvh in range(2):
        m_ref[kvh] = _row_max_rep(m_ref[kvh])
        acc_ref[kvh] = _dot(jnp.exp(s_diag[kvh] - _rep2(m_ref[kvh])).astype(BF16), _with_ones(kv_diag[kvh]))

    def acc_tile(j):
        for kvh in range(2):
            s, kvt = past_scores(j, kvh)
            acc_ref[kvh] += _dot(jnp.exp(s - _rep2(m_ref[kvh])).astype(BF16), _with_ones(kvt))

    _pair_loop(i, acc_tile)
    for kvh in range(2):
        out = (acc_ref[kvh, :, 0:SLAB] / jnp.maximum(acc_ref[kvh, :, SLAB:2 * SLAB], 1e-30)).astype(BF16)
        o_ref[0, :, SLAB * (2 * kvh):SLAB * (2 * kvh + 1)] = out[0:tq]
        o_ref[0, :, SLAB * (2 * kvh + 1):SLAB * (2 * kvh + 2)] = out[tq:2 * tq]


def _moba_prompt(q, kvb, kbar, bt, far):
    nb, t, _ = q.shape
    tq = TILE
    assert tq == MOBA_BLOCK
    full = lambda a: pl.BlockSpec(a.shape, lambda n, i: (0,) * a.ndim)
    return pl.pallas_call(
        _moba_prompt_kernel, grid=(nb, t // tq),
        in_specs=[pl.BlockSpec((1, tq, 512), lambda n, i: (n, i, 0)),
                  pl.BlockSpec((1, t, 256), lambda n, i: (n, 0, 0)),
                  pl.BlockSpec((1,) + kbar.shape[1:], lambda n, i: (n, 0, 0)),
                  full(bt), full(far)],
        out_specs=pl.BlockSpec((1, tq, 512), lambda n, i: (n, i, 0)),
        out_shape=jax.ShapeDtypeStruct((nb, t, 512), BF16),
        scratch_shapes=[pltpu.VMEM((2, 2 * tq, 128), F32), pltpu.VMEM((2, 2 * tq, 2 * SLAB), F32),
                        pltpu.VMEM((2, MOBA_TOP, 2 * tq, 128), F32)],
        compiler_params=_cparams(("parallel", "arbitrary")),
        name="moba_prompt",
    )(q, kvb, kbar, bt, far)


def _cmp_a_kernel(x_ref, w_ref, o_ref):
    o_ref[0] = _dot(x_ref[0].astype(BF16), w_ref[...])


def _cmp_a(x, w_c1):
    nb, j1, c = x.shape
    tj = min(TILE, j1)
    return pl.pallas_call(
        _cmp_a_kernel, grid=(nb, j1 // tj),
        in_specs=[pl.BlockSpec((1, tj, c), lambda n, i: (n, i, 0)),
                  pl.BlockSpec(w_c1.shape, lambda n, i: (0, 0), pipeline_mode=pl.Buffered(1))],
        out_specs=pl.BlockSpec((1, tj, 512), lambda n, i: (n, i, 0)),
        out_shape=jax.ShapeDtypeStruct((nb, j1, 512), F32),
        compiler_params=_cparams(("parallel", "parallel")),
        name="cmp_a",
    )(x, w_c1)


def _gelu_tanh(x):
    return 0.5 * x * (1.0 + jnp.tanh(math.sqrt(2.0 / math.pi) * (x + 0.044715 * (x * x * x))))


def _cmp_b_kernel(fs_ref, pe_ref, w1_ref, w2_ref, o_ref):
    j1 = fs_ref.shape[1]
    pe = _dot(pe_ref[...].astype(BF16), w1_ref[...])
    pe_term = pe[0:1, 0:256] + pe[1:2, 256:512]
    first = fs_ref[0, :, 0:256]
    nxt = pltpu.roll(fs_ref[0, :, 256:512], j1 - 1, 0)
    h = _gelu_tanh(first + nxt + pe_term)
    o_ref[0] = _dot(h.astype(BF16), w2_ref[...]).astype(BF16)


def _cmp_b(fs, pe_rows, w_c1, w_c2):
    nb, j1, _ = fs.shape
    full = lambda a: pl.BlockSpec(a.shape, lambda n: (0,) * a.ndim)
    return pl.pallas_call(
        _cmp_b_kernel, grid=(nb,),
        in_specs=[pl.BlockSpec((1, j1, 512), lambda n: (n, 0, 0)), full(pe_rows), full(w_c1), full(w_c2)],
        out_specs=pl.BlockSpec((1, j1, SLAB), lambda n: (n, 0, 0)),
        out_shape=jax.ShapeDtypeStruct((nb, j1, SLAB), BF16),
        compiler_params=_cparams(("parallel",)),
        name="cmp_b",
    )(fs, pe_rows, w_c1, w_c2)


def _lut_gather(lut_row, idx):
    m, c = idx.shape
    src = jnp.broadcast_to(lut_row, (m, 128))
    return jnp.concatenate([jnp.take_along_axis(src, idx[:, 128 * k:128 * (k + 1)], axis=1)
                            for k in range(c // 128)], axis=1)


def _masked_softmax(s, valid):
    m = jnp.max(s, axis=1, keepdims=True)
    e = jnp.where(valid, jnp.exp(s - m), 0.0)
    return e / jnp.maximum(jnp.sum(e, axis=1, keepdims=True), 1e-30)


def _select_blocks_t(imp_t, qpos):
    b = lax.broadcasted_iota(jnp.int32, imp_t.shape, 0)
    cur = qpos >> 6
    forced = (b == 0) | (b == cur) | (b == cur - 1)
    score = jnp.where(forced, -NEG, jnp.where(b <= cur, imp_t, NEG))
    sel = jnp.zeros(imp_t.shape, F32)
    for _ in range(SEL_TOP):
        m = jnp.max(score, axis=0, keepdims=True)
        first = jnp.min(jnp.where(score == m, b, 1 << 20), axis=0, keepdims=True)
        hit = (b == first) & (m > 0.5 * NEG)
        sel = jnp.where(hit, 1.0, sel)
        score = jnp.where(hit, NEG, score)
    return sel


def _nsa_prompt_kernel(q_ref, kv_ref, cmp_ref, g_ref, lut_ref, cover_ref, e_ref, bt_ref, far_ref, o_ref,
                       oc_ref, os_ref, mx_ref, acc2_ref):
    tq = q_ref.shape[1]
    j1 = cmp_ref.shape[1]
    i = pl.program_id(1)
    nh = N_HEADS
    q4 = jnp.concatenate([q_ref[0, :, SLAB * h:SLAB * (h + 1)] for h in range(nh)], axis=0)
    rowq = lax.broadcasted_iota(jnp.int32, (tq, tq), 0)
    col = lax.broadcasted_iota(jnp.int32, (tq, tq), 1)
    rep = lambda x: jnp.concatenate([x] * nh, axis=0)
    qpos = i * tq + lax.broadcasted_iota(jnp.int32, (tq, 1), 0)

    cmp = cmp_ref[0]
    jcol = lax.broadcasted_iota(jnp.int32, (tq, j1), 1)
    rel_c = qpos - (jcol * CMP_STRIDE + (CMP_LEN - 1))
    valid = (rel_c >= 0) & (jcol < j1 - 1)
    idx = jnp.clip(rel_c, 0, 127)
    bias = jnp.concatenate([_lut_gather(lut_ref[h:h + 1, :], idx) for h in range(nh)], axis=0)
    valid4 = rep(valid)
    pc = _masked_softmax(jnp.where(valid4, _dot_nt(q4, cmp) + bias, NEG), valid4)
    oc_ref[...] = _dot(pc.astype(BF16), cmp)
    psum = pc[0:tq] + pc[tq:2 * tq] + pc[2 * tq:3 * tq] + pc[3 * tq:4 * tq]
    p_hi = psum.astype(BF16)
    p_lo = (psum - p_hi.astype(F32)).astype(BF16)
    imp_t = _dot_nt(cover_ref[...], p_hi) + _dot_nt(cover_ref[...], p_lo)
    qpos_row = i * tq + lax.broadcasted_iota(jnp.int32, (1, tq), 1)
    selb = _select_blocks_t(imp_t, qpos_row).T.astype(BF16)

    def kv_tile(j, part):
        return kv_ref[0, pl.ds(pl.multiple_of(j * tq, tq), tq), SLAB * part:SLAB * (part + 1)]

    far2 = jnp.broadcast_to(far_ref[...], (nh * tq, tq))

    def past_scores(j):
        kvt = kv_tile(j, 1)
        bias = jnp.where(j == i - 1, bt_ref[1], far2)
        ok = _dot(selb, e_ref[j]) > 0.5
        return jnp.where(rep(ok), _dot_nt(q4, kvt) + bias, NEG), kvt

    kvt = kv_tile(i, 1)
    ok = (_dot(selb, e_ref[i]) > 0.5) & (col <= rowq)
    s_diag = jnp.where(rep(ok), _dot_nt(q4, kvt) + bt_ref[0], NEG)
    mx_ref[...] = _lane_max(s_diag)

    def max_tile(j):
        mx_ref[...] = jnp.maximum(mx_ref[...], _lane_max(past_scores(j)[0]))

    def acc_tile(j):
        s, kvt = past_scores(j)
        acc2_ref[...] += _dot(jnp.exp(s - _rep2(mx_ref[...])).astype(BF16), _with_ones(kvt))

    _pair_loop(i, max_tile)
    mx_ref[...] = _row_max_rep(mx_ref[...])
    acc2_ref[...] = _dot(jnp.exp(s_diag - _rep2(mx_ref[...])).astype(BF16), _with_ones(kvt))
    _pair_loop(i, acc_tile)
    os_ref[...] = acc2_ref[:, 0:SLAB] / jnp.maximum(acc2_ref[:, SLAB:2 * SLAB], 1e-30)

    kv0, kv1, kv2 = kv_tile(i, 2), kv_tile(jnp.maximum(i - 1, 0), 2), kv_tile(jnp.maximum(i - 2, 0), 2)
    s0 = jnp.where(rep(col <= rowq), _dot_nt(q4, kv0) + bt_ref[0], NEG)
    s1 = jnp.where(i >= 1, _dot_nt(q4, kv1) + bt_ref[1], NEG)
    s2 = jnp.where(rep(col > rowq) & (i >= 2), _dot_nt(q4, kv2) + far2, NEG)
    mw = _row_max_rep(jnp.maximum(jnp.maximum(_lane_max(s0), _lane_max(s1)), _lane_max(s2)))
    mw2 = _rep2(mw)
    accw = (_dot(jnp.exp(s0 - mw2).astype(BF16), _with_ones(kv0))
            + _dot(jnp.exp(s1 - mw2).astype(BF16), _with_ones(kv1))
            + _dot(jnp.exp(s2 - mw2).astype(BF16), _with_ones(kv2)))
    ow = accw[:, 0:SLAB] / jnp.maximum(accw[:, SLAB:2 * SLAB], 1e-30)
    g = g_ref[0]
    for h in range(nh):
        r = slice(h * tq, (h + 1) * tq)
        o = (g[:, h:h + 1] * oc_ref[r, :] + g[:, nh + h:nh + h + 1] * os_ref[r, :]
             + g[:, 2 * nh + h:2 * nh + h + 1] * ow[r])
        o_ref[0, :, SLAB * h:SLAB * (h + 1)] = o.astype(BF16)


def _cover(j1, nb_pad):
    j = np.arange(j1)[:, None]
    b = np.arange(nb_pad)[None, :]
    cstart, cend, bstart = j * CMP_STRIDE, j * CMP_STRIDE + CMP_LEN - 1, b * SEL_BLOCK
    return jnp.asarray(((cstart < bstart + SEL_BLOCK) & (cend >= bstart)).astype(np.float32), dtype=BF16)


def _expand(n_tiles, tk, nb_pad):
    key = np.arange(n_tiles * tk).reshape(n_tiles, 1, tk)
    b = np.arange(nb_pad).reshape(1, nb_pad, 1)
    return jnp.asarray((key // SEL_BLOCK == b).astype(np.float32), dtype=BF16)


def _nsa_prompt(q, kvb, cmp, gates, lut, bt, far):
    nb, t, _ = q.shape
    tq = TILE
    assert WINDOW == 2 * tq and t // SEL_BLOCK <= 128 and t // SEL_BLOCK >= SEL_TOP
    j1 = cmp.shape[1]
    cover = _cover(j1, 128).T
    e = _expand(t // tq, tq, 128)
    full = lambda a: pl.BlockSpec(a.shape, lambda n, i: (0,) * a.ndim)
    return pl.pallas_call(
        _nsa_prompt_kernel, grid=(nb, t // tq),
        in_specs=[pl.BlockSpec((1, tq, 512), lambda n, i: (n, i, 0)),
                  pl.BlockSpec((1, t, 384), lambda n, i: (n, 0, 0)),
                  pl.BlockSpec((1, j1, SLAB), lambda n, i: (n, 0, 0)),
                  pl.BlockSpec((1, tq, 128), lambda n, i: (n, i, 0)),
                  full(lut), full(cover), full(e), full(bt), full(far)],
        out_specs=pl.BlockSpec((1, tq, 512), lambda n, i: (n, i, 0)),
        out_shape=jax.ShapeDtypeStruct((nb, t, 512), BF16),
        scratch_shapes=[pltpu.VMEM((N_HEADS * tq, SLAB), F32), pltpu.VMEM((N_HEADS * tq, SLAB), F32),
                        pltpu.VMEM((N_HEADS * tq, 128), F32), pltpu.VMEM((N_HEADS * tq, 2 * SLAB), F32)],
        compiler_params=_cparams(("parallel", "arbitrary")),
        name="nsa_prompt",
    )(q, kvb, cmp, gates, lut, cover, e, bt, far)


def kernel(x_prompt, x_sample, cache_sb_kv, cache_nsa_kv, state_nsa_win, cache_mla, cache_moba_kv, page_table, c_prompt, c_sample, rel_bias, ln_in_g, ln_in_b, w_ada, b_ada, w_in, nsa_cmp_pe, nsa_cmp_w1, nsa_cmp_w2, mla_q_norm, mla_kv_norm, mla_w_uq, mla_w_uk, mla_w_uv, w_br, w_o, ln1_g, ln1_b, w_ff1, w_ff3, w_ff2, ln2_g, ln2_b):
    w = dict(w_ada=w_ada, b_ada=b_ada, w_in=w_in, nsa_cmp_pe=nsa_cmp_pe, nsa_cmp_w1=nsa_cmp_w1, nsa_cmp_w2=nsa_cmp_w2,
             mla_q_norm=mla_q_norm, mla_kv_norm=mla_kv_norm, mla_w_uq=mla_w_uq, mla_w_uk=mla_w_uk,
             mla_w_uv=mla_w_uv, w_br=w_br, w_o=w_o, ln1_g=ln1_g, ln1_b=ln1_b, w_ff1=w_ff1, w_ff3=w_ff3,
             w_ff2=w_ff2, ln2_g=ln2_g, ln2_b=ln2_b, ln_in_g=ln_in_g, ln_in_b=ln_in_b, rel_bias=rel_bias)
    packs = [_pack_layer(l, w_in, mla_w_uq, mla_w_uk, mla_w_uv, w_br, nsa_cmp_pe, nsa_cmp_w1, nsa_cmp_w2)
             for l in range(DEPTH)]
    tabs = _bias_tables(rel_bias)
    y_p, st_p = _prompt_trunk(x_prompt, c_prompt, w, packs, tabs)
    pools = (cache_sb_kv, cache_nsa_kv, state_nsa_win, cache_mla, cache_moba_kv)
    y_s, st_s = _decode_trunk(x_sample, c_sample, pools, page_table, w, packs, tabs)
    out = [y_p, y_s]
    for a, b in zip(st_p, st_s):
        out += [a, b]
    return tuple(out)


def _decode_trunk(x, c, pools, page_table, w, packs, tabs):
    n_seq, tt, _ = x.shape
    assert tt <= 8 and tt & (tt - 1) == 0
    sb_pool, nsa_pool, win_state, mla_pool, moba_pool = pools
    n_pool = sb_pool.shape[1]
    n_pages = page_table.shape[1]
    past_len = n_pages * PAGE
    assert past_len % MOBA_BLOCK == 0 and past_len >= WINDOW and past_len // SEL_BLOCK + 1 <= 256
    n_chunks = n_pages // G_PAGES
    rows = 8 * N_HEADS
    lut, _ = tabs
    far_nsa = jnp.repeat(w["rel_bias"][NUM_BUCKETS - 1, 0:N_HEADS], 8).reshape(rows, 1)
    far_moba = jnp.repeat(w["rel_bias"][NUM_BUCKETS - 1, N_HEADS:2 * N_HEADS], 8).reshape(rows, 1)
    feat_major = lambda p: jnp.transpose(p, (0, 1, 3, 4, 5, 2)).reshape(DEPTH, n_pool, 256, PAGE)
    sb_pool = feat_major(sb_pool)
    nsa_pool4 = feat_major(nsa_pool)
    moba_pool = feat_major(moba_pool)
    mla_pool = jnp.transpose(mla_pool, (0, 1, 3, 2))
    win_state = win_state.reshape(DEPTH, n_seq, win_state.shape[2], SLAB)
    r = n_seq * tt
    xf = x.reshape(1, r, D_MODEL)
    per_seq = lambda a: a.reshape(n_seq, tt, a.shape[-1])
    new = ([], [], [], [], [])
    for l in range(DEPTH):
        pk = packs[l]
        mod = _ada(c, w["w_ada"][l], w["b_ada"][l])
        mod = jnp.repeat(mod, tt, axis=0).reshape(1, r, 6 * D_MODEL)
        res = _proj(xf, mod, w["ln_in_g"], w["ln_in_b"], pk, w["mla_q_norm"][l], w["mla_kv_norm"][l],
                    apply_ln=(l == 0), pos0=past_len, period=tt, mod_per_row=True, emit_kbar=False)
        (sbq, sbkv, sbkvb, nsaq, nsakv, nsawin, nsakvb, nsag, mlar, mlarb, mlaq, mobaq, mobakv, mobakvb) = res[:14]
        if l == 0:
            xf = res[14]
        o_sb = _sb_dec(page_table, sb_pool, l, per_seq(sbq), per_seq(sbkvb))
        o_mla = _mla_dec(page_table, mla_pool, l, per_seq(mlaq), per_seq(mlarb), pk["w_uv"])
        o_moba = _moba_dec(page_table, moba_pool, l, per_seq(mobaq), per_seq(mobakvb), lut, far_moba)
        fs = _cmp_a_dec(page_table, nsa_pool4, l, pk["w_c1d"])
        cmp = _cmp_b(fs, pk["pe_rows"], pk["w_c1"], pk["w_c2"])
        o_nsa = _nsa_dec(page_table, nsa_pool4, l, per_seq(nsaq), per_seq(nsakvb), cmp, per_seq(nsag), win_state[l],
                         lut, far_nsa)
        flat = lambda o: o.reshape(1, r, 512)
        xf = _dense_tail(l, xf, mod, True, (flat(o_sb), flat(o_nsa), flat(o_mla), flat(o_moba)), w, pk)
        wl = win_state.shape[2]
        win_new = jnp.concatenate([win_state[l], per_seq(nsawin)], axis=1)[:, tt:]
        rows_out = (sbkv.reshape(n_seq, tt, 2, 2, HD), nsakv.reshape(n_seq, tt, 2, 2, HD),
                    win_new.reshape(n_seq, wl, 2, HD), per_seq(mlar), mobakv.reshape(n_seq, tt, 2, 2, HD))
        for lst, a in zip(new, rows_out):
            lst.append(a)
    return xf.reshape(n_seq, tt, D_MODEL), [jnp.stack(lst, axis=0) for lst in new]


G_PAGES = 64
CHUNK_PITCH = 24


def _page_specs(l, feats, n_chunks, reverse, feat_block=0):
    def spec(g):
        if reverse:
            return pl.BlockSpec((None, None, feats, PAGE),
                                lambda n, c, pt: (l, pt[n, (n_chunks - 1 - c) * G_PAGES + g], feat_block, 0))
        return pl.BlockSpec((None, None, feats, PAGE), lambda n, c, pt: (l, pt[n, c * G_PAGES + g], feat_block, 0))
    return [spec(g) for g in range(G_PAGES)]


def _seq_spec(shape):
    nd = len(shape)
    return pl.BlockSpec((1,) + tuple(shape[1:]), lambda n, c, pt: (n,) + (0,) * (nd - 1))


def _const_spec(a):
    return pl.BlockSpec(a.shape, lambda n, c, pt: (0,) * a.ndim)


def _paged_call(kernel, name, page_table, cache, l, feats, reverse, seq_ins, const_ins, out_lanes, scratch,
                feat_block=0):
    n_seq, n_pages = page_table.shape
    assert n_pages % G_PAGES == 0
    n_chunks = n_pages // G_PAGES
    tt = seq_ins[0].shape[1]
    grid_spec = pltpu.PrefetchScalarGridSpec(
        num_scalar_prefetch=1, grid=(n_seq, n_chunks),
        in_specs=[_seq_spec(a.shape) for a in seq_ins] + [_const_spec(a) for a in const_ins]
        + _page_specs(l, feats, n_chunks, reverse, feat_block),
        out_specs=_seq_spec((n_seq, tt, out_lanes)),
        scratch_shapes=scratch)
    return pl.pallas_call(
        functools.partial(kernel, n_seq_ins=len(seq_ins), n_const=len(const_ins)),
        grid_spec=grid_spec, out_shape=jax.ShapeDtypeStruct((n_seq, tt, out_lanes), F32),
        compiler_params=_cparams(("parallel", "arbitrary")), name=name,
    )(page_table, *seq_ins, *const_ins, *([cache] * G_PAGES))


def _fill_rows(dst_ref, src, width, lane_of_head):
    dst_ref[...] = jnp.zeros(dst_ref.shape, F32)
    tt = src.shape[0]
    for h in range(N_HEADS):
        lo = lane_of_head(h)
        dst_ref[8 * h:8 * h + tt, lo:lo + width] = src[:, width * h:width * (h + 1)].astype(F32)


def _fill_new(dst_ref, src):
    dst_ref[...] = jnp.zeros(dst_ref.shape, F32)
    dst_ref[0:src.shape[0], :] = src.astype(F32)


def _row_t(shape):
    return lax.broadcasted_iota(jnp.int32, shape, 0) & 7


def _head_lut(lut_ref, head0, idx):
    return jnp.concatenate([_lut_gather(lut_ref[head0 + h:head0 + h + 1, :], idx[8 * h:8 * h + 8])
                            for h in range(N_HEADS)], axis=0)


def _sb_dec_kernel(pt_ref, *refs, n_seq_ins, n_const):
    q_ref, new_ref, tri_ref = refs[:3]
    pages = refs[3:3 + G_PAGES]
    o_ref, q32_ref, new32_ref, acc_ref, carry_ref = refs[3 + G_PAGES:]
    c = pl.program_id(1)
    tt = q_ref.shape[1]
    tri = tri_ref[...]
    rt = _row_t((8 * N_HEADS, PAGE))
    col = lax.broadcasted_iota(jnp.int32, (8 * N_HEADS, PAGE), 1)

    @pl.when(c == 0)
    def _():
        _fill_rows(q32_ref, q_ref[0], SLAB, lambda h: SLAB * (h // 2))
        _fill_new(new32_ref, new_ref[0])
        kvt = new32_ref[...].astype(BF16)
        z = _dot_nt(q32_ref[...].astype(BF16), kvt)
        contrib, carry = _sb_tile(z, kvt, tri, jnp.zeros((8 * N_HEADS, 1), F32), col < rt)
        acc_ref[...] = contrib
        carry_ref[...] = carry

    q = q32_ref[...].astype(BF16)
    kv = [pages[g][...].astype(BF16) for g in range(G_PAGES)]
    zs = [_dot(q, kv[g]) for g in range(G_PAGES)]
    lrs = [_neg_softplus(z) for z in zs]
    tails = _split_dot(jnp.concatenate(lrs, axis=0), tri)
    carry = carry_ref[...]
    ws = [None] * G_PAGES
    for g in reversed(range(G_PAGES)):
        tail = tails[8 * N_HEADS * g:8 * N_HEADS * (g + 1)]
        ws[g] = jnp.exp(zs[g] + lrs[g] + tail + carry).astype(BF16)
        carry = carry + tail[:, 0:1] + lrs[g][:, 0:1]
    carry_ref[...] = carry
    acc_ref[...] += _dot_nt(jnp.concatenate(ws, axis=1), jnp.concatenate(kv, axis=1))

    @pl.when(c == pl.num_programs(1) - 1)
    def _():
        acc = acc_ref[...]
        for h in range(N_HEADS):
            o_ref[0, :, SLAB * h:SLAB * (h + 1)] = acc[8 * h:8 * h + tt, SLAB * (h // 2):SLAB * (h // 2 + 1)]


def _mla_dec_kernel(pt_ref, *refs, n_seq_ins, n_const):
    q_ref, new_ref, wuv_ref = refs[:3]
    pages = refs[3:3 + G_PAGES]
    o_ref, q32_ref, new32_ref, m_ref, l_ref, acc_ref = refs[3 + G_PAGES:]
    c = pl.program_id(1)
    tt = q_ref.shape[1]
    rt = _row_t((8 * N_HEADS, PAGE))
    col = lax.broadcasted_iota(jnp.int32, (8 * N_HEADS, PAGE), 1)

    @pl.when(c == 0)
    def _():
        _fill_rows(q32_ref, q_ref[0], 256, lambda h: 0)
        _fill_new(new32_ref, new_ref[0])
        rows = new32_ref[...].astype(BF16)
        s = jnp.where((col <= rt) & (col < tt), _dot_nt(q32_ref[...].astype(BF16), rows), NEG)
        _softmax_step(s, rows[:, 0:KV_LORA], m_ref, l_ref, acc_ref, True)

    q = q32_ref[...].astype(BF16)
    ss, vs = [], []
    for g in range(G_PAGES):
        lat = pages[g][0:KV_LORA, :].astype(BF16)
        ss.append(_dot(q[:, 0:KV_LORA], lat)
                  + _dot(q[:, KV_LORA:KV_LORA + ROPE], pages[g][KV_LORA:KV_LORA + ROPE, :].astype(BF16)))
        vs.append(lat)
    _softmax_step(jnp.concatenate(ss, axis=1), jnp.concatenate(vs, axis=1), m_ref, l_ref, acc_ref, False,
                  v_is_transposed=True)

    @pl.when(c == pl.num_programs(1) - 1)
    def _():
        olat = (acc_ref[...] / jnp.maximum(l_ref[...], 1e-30)).astype(BF16)
        for h in range(N_HEADS):
            o_ref[0, :, SLAB * h:SLAB * (h + 1)] = _dot(olat[8 * h:8 * h + 8], wuv_ref[h])[0:tt]


def _moba_dec_kernel(pt_ref, *refs, n_seq_ins, n_const, past_len):
    q_ref, new_ref, lut_ref, far_ref = refs[:4]
    pages = refs[4:4 + G_PAGES]
    (o_ref, q32_ref, new32_ref, gs_ref, mb_ref, lb_ref, accb_ref, blast_ref) = refs[4 + G_PAGES:]
    c = pl.program_id(1)
    n_chunks = gs_ref.shape[0]
    bpc = G_PAGES // 2
    tt = q_ref.shape[1]
    rows = 8 * N_HEADS
    rt = _row_t((rows, PAGE))
    col = lax.broadcasted_iota(jnp.int32, (rows, PAGE), 1)
    far = far_ref[...]

    @pl.when(c == 0)
    def _():
        _fill_rows(q32_ref, q_ref[0], SLAB, lambda h: SLAB * (h // 2))
        _fill_new(new32_ref, new_ref[0])
        gs_ref[...] = jnp.zeros(gs_ref.shape, F32)
        mb_ref[...] = jnp.zeros(mb_ref.shape, F32)
        lb_ref[...] = jnp.zeros(lb_ref.shape, F32)
        t2 = _row_t((rows, MOBA_BLOCK))
        s2 = lax.broadcasted_iota(jnp.int32, (rows, MOBA_BLOCK), 1)
        blast_ref[...] = _head_lut(lut_ref, N_HEADS, jnp.clip(MOBA_BLOCK + t2 - s2, 0, 127))

    q = q32_ref[...].astype(BF16)
    for k in range(bpc):
        kv = jnp.concatenate([pages[2 * k][...], pages[2 * k + 1][...]], axis=1).astype(BF16)
        is_last = (c == n_chunks - 1) & (k == bpc - 1)
        raw = _dot(q, kv)
        s = raw + jnp.where(is_last, blast_ref[...], far)
        m = jnp.max(s, axis=1, keepdims=True)
        p = jnp.exp(s - m)
        gs_ref[c, :, k:k + 1] = jnp.sum(raw, axis=1, keepdims=True) * (1.0 / MOBA_BLOCK)
        mb_ref[c, :, k:k + 1] = m
        lb_ref[c, :, k:k + 1] = jnp.sum(p, axis=1, keepdims=True)
        accb_ref[c * bpc + k] = _dot_nt(p.astype(BF16), kv)

    @pl.when(c == n_chunks - 1)
    def _():
        lane = lax.broadcasted_iota(jnp.int32, (rows, 128), 1)

        def gather(ref):
            out = jnp.zeros((rows, 128), F32)
            for cc in range(n_chunks):
                part = jnp.where(lane < bpc, ref[cc], 0.0)
                out = out + (part if cc == 0 else pltpu.roll(part, cc * bpc, 1))
            return out

        nblk = n_chunks * bpc
        gs = jnp.where(lane < nblk, gather(gs_ref), NEG)
        ids = _topk_ids(gs, MOBA_TOP)
        sel = (lane == ids[0]) | (lane == ids[1]) | (lane == ids[2])
        m_all = gather(mb_ref)
        l_all = gather(lb_ref)
        kvn = new32_ref[...].astype(BF16)
        bias_n = _head_lut(lut_ref, N_HEADS, jnp.clip(rt - col, 0, 127))
        s_n = jnp.where((col <= rt) & (col < tt), _dot_nt(q, kvn) + bias_n, NEG)
        m_n = jnp.max(s_n, axis=1, keepdims=True)
        m_fin = jnp.maximum(m_n, jnp.max(jnp.where(sel, m_all, NEG), axis=1, keepdims=True))
        p_n = jnp.exp(s_n - m_fin)
        wgt = jnp.where(sel, jnp.exp(m_all - m_fin), 0.0)
        l_fin = jnp.sum(p_n, axis=1, keepdims=True) + jnp.sum(wgt * l_all, axis=1, keepdims=True)
        acc = _dot(p_n.astype(BF16), kvn)
        for b in range(nblk):
            acc = acc + wgt[:, b:b + 1] * accb_ref[b]
        out = acc / jnp.maximum(l_fin, 1e-30)
        for h in range(N_HEADS):
            o_ref[0, :, SLAB * h:SLAB * (h + 1)] = out[8 * h:8 * h + tt, SLAB * (h // 2):SLAB * (h // 2 + 1)]


def _cmp_a_dec_kernel(pt_ref, *refs):
    w_ref = refs[0]
    pages = refs[1:1 + G_PAGES]
    o_ref, xt_ref = refs[1 + G_PAGES:]
    chunks = []
    n_chunk = PAGE // CMP_STRIDE
    for g in range(G_PAGES):
        xt = pages[g][...].T
        for j in range(n_chunk):
            xt_ref[CHUNK_PITCH * j:CHUNK_PITCH * j + CMP_STRIDE, :] = xt[CMP_STRIDE * j:CMP_STRIDE * (j + 1), :]
        chunks.append(jnp.concatenate([xt_ref[pl.ds(s, n_chunk, stride=CHUNK_PITCH), :]
                                       for s in range(CMP_STRIDE)], axis=1))
    o_ref[0] = _dot(jnp.concatenate(chunks, axis=0).astype(BF16), w_ref[...])


def _cmp_a_dec(page_table, pool_t, l, w_c1d):
    n_seq, n_pages = page_table.shape
    n_chunks = n_pages // G_PAGES
    rows = PAGE // CMP_STRIDE
    grid_spec = pltpu.PrefetchScalarGridSpec(
        num_scalar_prefetch=1, grid=(n_seq, n_chunks),
        in_specs=[_const_spec(w_c1d)]
        + [pl.BlockSpec((None, None, SLAB, PAGE), lambda n, c, pt, g=g: (l, pt[n, c * G_PAGES + g], 0, 0))
           for g in range(G_PAGES)],
        out_specs=pl.BlockSpec((1, rows * G_PAGES, 512), lambda n, c, pt: (n, c, 0)),
        scratch_shapes=[pltpu.VMEM((CHUNK_PITCH * rows, SLAB), F32)])
    return pl.pallas_call(
        _cmp_a_dec_kernel, grid_spec=grid_spec,
        out_shape=jax.ShapeDtypeStruct((n_seq, rows * n_pages, 512), F32),
        compiler_params=_cparams(("parallel", "arbitrary")), name="cmp_a_dec",
    )(page_table, w_c1d, *([pool_t] * G_PAGES))


def _nsa_dec_kernel(pt_ref, *refs, n_seq_ins, n_const, past_len):
    q_ref, new_ref, cmp_ref, g_ref, win_ref, lut_ref, cover_ref, far_ref = refs[:8]
    pages = refs[8:8 + G_PAGES]
    (o_ref, q32_ref, newsel_ref, newwin_ref, g32_ref, sel_ref, blast_ref, m_ref, l_ref, acc_ref,
     oc_ref, ow_ref) = refs[8 + G_PAGES:]
    c = pl.program_id(1)
    n_chunks = sel_ref.shape[0]
    n_pages = n_chunks * G_PAGES
    tt = q_ref.shape[1]
    nh = N_HEADS
    rows = 8 * nh
    rt = _row_t((rows, PAGE))
    col = lax.broadcasted_iota(jnp.int32, (rows, PAGE), 1)
    far = far_ref[...]
    rep = lambda x: jnp.concatenate([x] * nh, axis=0)

    @pl.when(c == 0)
    def _():
        _fill_rows(q32_ref, q_ref[0], SLAB, lambda h: 0)
        new = new_ref[0]
        _fill_new(newsel_ref, new[:, SLAB:2 * SLAB])
        _fill_new(newwin_ref, new[:, 2 * SLAB:3 * SLAB])
        g32_ref[...] = jnp.zeros(g32_ref.shape, F32)
        for h in range(nh):
            g32_ref[8 * h:8 * h + tt, :] = g_ref[0]
        q = q32_ref[...].astype(BF16)
        cmp = cmp_ref[0]
        j1 = cmp.shape[0]
        jcol = lax.broadcasted_iota(jnp.int32, (rows, j1), 1)
        rel_c = past_len + _row_t((rows, j1)) - (jcol * CMP_STRIDE + (CMP_LEN - 1))
        valid = jcol < j1 - 1
        bias = _head_lut(lut_ref, 0, jnp.clip(rel_c, 0, 127))
        pc = _masked_softmax(jnp.where(valid, _dot_nt(q, cmp) + bias, NEG), valid)
        oc_ref[...] = _dot(pc.astype(BF16), cmp)
        psum = pc[0:8] + pc[8:16] + pc[16:24] + pc[24:32]
        psum = jnp.concatenate([psum, jnp.zeros((128 - 8, j1), F32)], axis=0)
        p_hi = psum.astype(BF16)
        p_lo = (psum - p_hi.astype(F32)).astype(BF16)
        imp_t = _dot_nt(cover_ref[...], p_hi) + _dot_nt(cover_ref[...], p_lo)
        qpos = past_len + lax.broadcasted_iota(jnp.int32, (1, 128), 1)
        sel = _select_blocks_t(imp_t, qpos).T[0:8]
        per = 2 * G_PAGES
        sel_ref[...] = jnp.zeros(sel_ref.shape, F32)
        for cc in range(n_chunks):
            sel_ref[cc, :, 0:per] = sel[:, cc * per:(cc + 1) * per]
        blast_ref[...] = _head_lut(lut_ref, 0, jnp.clip(PAGE + rt - col, 0, 127))
        wst = win_ref[0].astype(BF16)
        wl = wst.shape[0]
        tw = _row_t((rows, wl))
        iw = lax.broadcasted_iota(jnp.int32, (rows, wl), 1)
        s = _dot_nt(q, wst) + _head_lut(lut_ref, 0, jnp.clip(wl + tw - iw, 0, 127))
        _softmax_step(jnp.where(iw > tw, s, NEG), wst, m_ref, l_ref, acc_ref, True)
        bias_n = _head_lut(lut_ref, 0, jnp.clip(rt - col, 0, 127))
        ok_n = (col <= rt) & (col < tt)
        wn = newwin_ref[...].astype(BF16)
        _softmax_step(jnp.where(ok_n, _dot_nt(q, wn) + bias_n, NEG), wn, m_ref, l_ref, acc_ref, False)
        ow_ref[...] = acc_ref[...] / jnp.maximum(l_ref[...], 1e-30)
        sn = newsel_ref[...].astype(BF16)
        _softmax_step(jnp.where(ok_n, _dot_nt(q, sn) + bias_n, NEG), sn, m_ref, l_ref, acc_ref, True)

    q = q32_ref[...].astype(BF16)
    selc = sel_ref[c]
    ss, vs = [], []
    for g in range(G_PAGES):
        kv = pages[g][...].astype(BF16)
        chosen = jnp.where(col[0:8] < SEL_BLOCK, selc[:, 2 * g:2 * g + 1], selc[:, 2 * g + 1:2 * g + 2]) > 0.5
        is_last = (c == n_chunks - 1) & (g == G_PAGES - 1)
        s = _dot(q, kv) + jnp.where(is_last, blast_ref[...], far)
        ss.append(jnp.where(rep(chosen), s, NEG))
        vs.append(kv)
    _softmax_step(jnp.concatenate(ss, axis=1), jnp.concatenate(vs, axis=1), m_ref, l_ref, acc_ref, False,
                  v_is_transposed=True)

    @pl.when(c == n_chunks - 1)
    def _():
        osel = acc_ref[...] / jnp.maximum(l_ref[...], 1e-30)
        lane = lax.broadcasted_iota(jnp.int32, (rows, 128), 1)
        head = lax.broadcasted_iota(jnp.int32, (rows, 128), 0) >> 3
        g32 = g32_ref[...]
        gcol = lambda k: jnp.sum(jnp.where(lane == k * nh + head, g32, 0.0), axis=1, keepdims=True)
        out = gcol(0) * oc_ref[...] + gcol(1) * osel + gcol(2) * ow_ref[...]
        for h in range(nh):
            o_ref[0, :, SLAB * h:SLAB * (h + 1)] = out[8 * h:8 * h + tt]


_ROWS = 8 * N_HEADS


def _sb_dec(page_table, pool_t, l, q, new):
    vm = pltpu.VMEM
    return _paged_call(
        _sb_dec_kernel, "sb_dec", page_table, pool_t, l, 256, True, [q, new], [_tri(PAGE)], 512,
        [vm((_ROWS, 256), F32), vm((PAGE, 256), F32), vm((_ROWS, 256), F32), vm((_ROWS, 1), F32)])


def _mla_dec(page_table, pool_t, l, q, new, w_uv):
    vm = pltpu.VMEM
    return _paged_call(
        _mla_dec_kernel, "mla_dec", page_table, pool_t, l, KV_LORA + ROPE, False, [q, new], [w_uv], 512,
        [vm((_ROWS, 256), F32), vm((PAGE, 256), F32), vm((_ROWS, 1), F32), vm((_ROWS, 1), F32),
         vm((_ROWS, KV_LORA), F32)])


def _moba_dec(page_table, pool_t, l, q, new, lut, far):
    vm = pltpu.VMEM
    n_pages = page_table.shape[1]
    n_chunks = n_pages // G_PAGES
    return _paged_call(
        functools.partial(_moba_dec_kernel, past_len=n_pages * PAGE), "moba_dec", page_table, pool_t, l, 256, False,
        [q, new], [lut, far], 512,
        [vm((_ROWS, 256), F32), vm((PAGE, 256), F32)] + [vm((n_chunks, _ROWS, 128), F32)] * 3
        + [vm((n_pages // 2, _ROWS, 256), F32), vm((_ROWS, MOBA_BLOCK), F32)])


def _nsa_dec(page_table, pool_t, l, q, new, cmp, gates, win_state, lut, far):
    vm = pltpu.VMEM
    n_pages = page_table.shape[1]
    n_chunks = n_pages // G_PAGES
    past_len = n_pages * PAGE
    cover = _cover(past_len // CMP_STRIDE, 256).T
    return _paged_call(
        functools.partial(_nsa_dec_kernel, past_len=past_len), "nsa_dec", page_table, pool_t, l, SLAB, False,
        [q, new, cmp, gates, win_state], [lut, cover, far], 512,
        [vm((_ROWS, SLAB), F32), vm((PAGE, SLAB), F32), vm((PAGE, SLAB), F32), vm((_ROWS, 128), F32),
         vm((n_chunks, 8, 128), F32), vm((_ROWS, PAGE), F32), vm((_ROWS, 1), F32), vm((_ROWS, 1), F32),
         vm((_ROWS, SLAB), F32), vm((_ROWS, SLAB), F32), vm((_ROWS, SLAB), F32)],
        feat_block=1)


def _dense_tail(l, x, mod, mod_per_row, outs, w, pk):
    row = lambda a: a[l].reshape(1, -1)
    x = _row_call(_merge_kernel, "merge", x, mod, mod_per_row, list(outs),
                  [pk["w_gate"], pk["w_brp"], w["w_o"][l].astype(BF16), row(w["ln1_g"]), row(w["ln1_b"])])
    return _row_call(_ffn_kernel, "ffn", x, mod, mod_per_row, [],
                     [w["w_ff1"][l].astype(BF16), w["w_ff3"][l].astype(BF16), w["w_ff2"][l].astype(BF16),
                      row(w["ln2_g"]), row(w["ln2_b"])])


def _prompt_trunk(x, c, w, packs, tabs):
    nb, t, _ = x.shape
    lut, tiles = tabs
    bt_nsa, far_nsa = _group_tiles(tiles[0:4], w["rel_bias"][NUM_BUCKETS - 1, 0:4], 1)
    bt_moba, far_moba = _group_tiles(tiles[4:8], w["rel_bias"][NUM_BUCKETS - 1, 4:8], 2)
    new = ([], [], [], [], [])
    for l in range(DEPTH):
        pk = packs[l]
        mod = _ada(c, w["w_ada"][l], w["b_ada"][l]).reshape(nb, 1, 6 * D_MODEL)
        res = _proj(x, mod, w["ln_in_g"], w["ln_in_b"], pk, w["mla_q_norm"][l], w["mla_kv_norm"][l],
                    apply_ln=(l == 0), pos0=0, period=t, mod_per_row=False, emit_kbar=True)
        (sbq, sbkv, sbkvb, nsaq, nsakv, nsawin, nsakvb, nsag, mlar, mlarb, mlaq, mobaq, mobakv, mobakvb) = res[:14]
        if l == 0:
            x = res[14]
        kbar = res[-1].reshape(nb, t // TILE, 256)
        kbar = jnp.pad(kbar, ((0, 0), (0, 128 - kbar.shape[1]), (0, 0)))
        o_sb = _sb_prompt(sbq, sbkvb)
        fs = _cmp_a(nsakv.reshape(nb, t // CMP_STRIDE, CMP_STRIDE * 256), pk["w_c1"])
        cmp = _cmp_b(fs, pk["pe_rows"], pk["w_c1"], pk["w_c2"])
        o_nsa = _nsa_prompt(nsaq, nsakvb, cmp, nsag, lut[0:4], bt_nsa[0], far_nsa[0])
        o_mla = _mla_prompt(mlaq, mlarb, pk["w_uv"])
        o_moba = _moba_prompt(mobaq, mobakvb, kbar, bt_moba, far_moba)
        x = _dense_tail(l, x, mod, False, (o_sb, o_nsa, o_mla, o_moba), w, pk)
        wl = min(WINDOW, t)
        rows = (sbkv.reshape(nb, t, 2, 2, HD), nsakv.reshape(nb, t, 2, 2, HD),
                nsawin[:, t - wl:].reshape(nb, wl, 2, HD), mlar, mobakv.reshape(nb, t, 2, 2, HD))
        for lst, r in zip(new, rows):
            lst.append(r)
    return x, [jnp.stack(lst, axis=0) for lst in new]
```
